```python
import math
import jax, jax.numpy as jnp
from jax import lax
import numpy as np

D_MODEL = 1024
BATCH = 16
SEQ = 4096
DEPTH = 1
DEC_BATCH = 8
DEC_SEQ = 32
PAST_LEN = 1024

CHUNK = 64
Q_BLOCK = 128
D_MIX = 2 * D_MODEL
D_SSM = D_MIX // 2
D_ATTN = D_MIX - D_SSM
SSM_HEAD_DIM = 64
N_SSM_HEADS = D_SSM // SSM_HEAD_DIM
N_GROUPS = 2
D_STATE = 128
D_BC = N_GROUPS * D_STATE
CONV_W = 4
D_XBC = D_SSM + 2 * D_BC
ATTN_HEAD_DIM = 64
N_ATTN_HEADS = D_ATTN // ATTN_HEAD_DIM
ATTN_SCALE = ATTN_HEAD_DIM ** -0.5
ALPHA = (2 * DEPTH) ** 0.25
BETA = (8 * DEPTH) ** -0.25
LN_EPS = 1e-5
RMS_EPS = 1e-5
Z_SSM_END = D_SSM
XBC_END = Z_SSM_END + D_XBC
DT_END = XBC_END + N_SSM_HEADS
Q_END = DT_END + D_ATTN
K_END = Q_END + D_ATTN
V_END = K_END + D_ATTN
Z_ATTN_END = V_END + D_ATTN
D_IN_PROJ = Z_ATTN_END + N_ATTN_HEADS

kernel_name = 'hymba_ssd_fox_streaming_step'


def causal_conv(xbc, conv_state, conv_w, conv_b):
    xp = jnp.concatenate([conv_state.astype(xbc.dtype), xbc], axis=1)
    y = lax.conv_general_dilated(xp, conv_w[:, None, :].astype(xbc.dtype), window_strides=(1,),
                                 padding='VALID', dimension_numbers=('NWC', 'WIO', 'NWC'),
                                 feature_group_count=xbc.shape[-1])
    return jax.nn.silu(y + conv_b.astype(xbc.dtype)), xp[:, xp.shape[1] - (CONV_W - 1):]


def ssd_scan(x, dt, a, b_in, c_in, h0, chunk):
    f32 = jnp.float32
    bsz, l = x.shape[0], x.shape[1]
    nc = l // chunk
    e = N_SSM_HEADS // N_GROUPS
    xc = x.astype(f32).reshape(bsz, nc, chunk, N_GROUPS, e, SSM_HEAD_DIM)
    dtc = dt.reshape(bsz, nc, chunk, N_GROUPS, e)
    bm = b_in.astype(f32).reshape(bsz, nc, chunk, N_GROUPS, D_STATE)
    cm = c_in.astype(f32).reshape(bsz, nc, chunk, N_GROUPS, D_STATE)
    cs = jnp.cumsum(dtc * a.reshape(N_GROUPS, e), axis=2)
    xdt = xc * dtc[..., None]
    tril = jnp.tril(jnp.ones((chunk, chunk), dtype=bool))
    diff = cs[:, :, :, None] - cs[:, :, None, :]
    decay = jnp.exp(jnp.where(tril[:, :, None, None], diff, -jnp.inf))
    cb = jnp.einsum('bcqgn,bcsgn->bcqsg', cm, bm)
    y_diag = jnp.einsum('bcqsg,bcqsge,bcsgep->bcqgep', cb, decay, xdt)
    decay_to_end = jnp.exp(cs[:, :, -1:] - cs)
    states = jnp.einsum('bcsgn,bcsge,bcsgep->bcgepn', bm, decay_to_end, xdt)
    chunk_decay = jnp.exp(cs[:, :, -1])
    h_init = h0.astype(f32).reshape(bsz, N_GROUPS, e, SSM_HEAD_DIM, D_STATE)

    def step(h, inp):
        s_c, d_c = inp
        return d_c[..., None, None] * h + s_c, h

    h_last, h_prev = lax.scan(step, h_init, (jnp.moveaxis(states, 1, 0), jnp.moveaxis(chunk_decay, 1, 0)))
    h_prev = jnp.moveaxis(h_prev, 0, 1)
    y_off = jnp.einsum('bcqgn,bcgepn,bcqge->bcqgep', cm, h_prev, jnp.exp(cs))
    y = (y_diag + y_off).reshape(bsz, l, N_SSM_HEADS, SSM_HEAD_DIM)
    return y, h_last.reshape(bsz, N_SSM_HEADS, SSM_HEAD_DIM, D_STATE)


def fox_attend(q, k, v, fq, fk, q_pos, k_pos):
    s = jnp.einsum('bqhd,bkhd->bhqk', q, k, preferred_element_type=jnp.float32) * ATTN_SCALE
    s = s + jnp.swapaxes(fq, 1, 2)[:, :, :, None] - jnp.swapaxes(fk, 1, 2)[:, :, None, :]
    s = jnp.where((k_pos[None, :] <= q_pos[:, None])[None, None], s, -jnp.inf)
    p = jax.nn.softmax(s, axis=-1)
    return jnp.einsum('bhqk,bkhd->bqhd', p.astype(v.dtype), v)


def hybrid_layer(x, conv_state, ssm_state, past_k, past_v, past_logf,
                 w_in, conv_w, conv_b, dt_bias, a_log, d_skip, ssm_norm_w, f_bias,
                 w_out, ln_g, ln_b, ssm_chunk, q_block):
    f32 = jnp.float32
    bsz, t = x.shape[0], x.shape[1]
    past = past_k.shape[1]
    proj = x @ w_in
    z_ssm = proj[..., :Z_SSM_END]
    xbc = proj[..., Z_SSM_END:XBC_END]
    dt_raw = proj[..., XBC_END:DT_END]
    q = proj[..., DT_END:Q_END].reshape(bsz, t, N_ATTN_HEADS, ATTN_HEAD_DIM)
    k = proj[..., Q_END:K_END].reshape(bsz, t, N_ATTN_HEADS, ATTN_HEAD_DIM)
    v = proj[..., K_END:V_END].reshape(bsz, t, N_ATTN_HEADS, ATTN_HEAD_DIM)
    z_attn = proj[..., V_END:Z_ATTN_END]
    f_raw = proj[..., Z_ATTN_END:]
    xbc, new_conv = causal_conv(xbc, conv_state, conv_w, conv_b)
    xs = xbc[..., :D_SSM].reshape(bsz, t, N_SSM_HEADS, SSM_HEAD_DIM)
    b_ssm = xbc[..., D_SSM:D_SSM + D_BC].reshape(bsz, t, N_GROUPS, D_STATE)
    c_ssm = xbc[..., D_SSM + D_BC:].reshape(bsz, t, N_GROUPS, D_STATE)
    dt = jax.nn.softplus(dt_raw.astype(f32) + dt_bias.astype(f32))
    a = -jnp.exp(a_log.astype(f32))
    y_ssm, new_ssm = ssd_scan(xs, dt, a, b_ssm, c_ssm, ssm_state, ssm_chunk)
    y_ssm = y_ssm + d_skip.astype(f32)[:, None] * xs.astype(f32)
    g = (y_ssm.reshape(bsz, t, D_SSM) * jax.nn.silu(z_ssm.astype(f32))).reshape(bsz, t, N_GROUPS, D_SSM // N_GROUPS)
    g = g * lax.rsqrt(jnp.mean(g * g, axis=-1, keepdims=True) + RMS_EPS)
    y_ssm = g.reshape(bsz, t, D_SSM) * ssm_norm_w.astype(f32)
    logf = jax.nn.log_sigmoid(f_raw.astype(f32) + f_bias.astype(f32))
    k_all = jnp.concatenate([past_k.astype(k.dtype), k], axis=1)
    v_all = jnp.concatenate([past_v.astype(v.dtype), v], axis=1)
    f_cum = jnp.cumsum(jnp.concatenate([past_logf.astype(f32), logf], axis=1), axis=1)
    fq = f_cum[:, past:]
    k_pos = jnp.arange(past + t)
    q_pos = past + jnp.arange(t)
    if q_block is None:
        o = fox_attend(q, k_all, v_all, fq, f_cum, q_pos, k_pos)
    else:
        nb = t // q_block
        qb = jnp.moveaxis(q.reshape(bsz, nb, q_block, N_ATTN_HEADS, ATTN_HEAD_DIM), 1, 0)
        fqb = jnp.moveaxis(fq.reshape(bsz, nb, q_block, N_ATTN_HEADS), 1, 0)
        o = lax.map(lambda blk: fox_attend(blk[0], k_all, v_all, blk[1], f_cum, blk[2], k_pos),
                    (qb, fqb, q_pos.reshape(nb, q_block)))
        o = jnp.moveaxis(o, 0, 1)
    y_attn = o.reshape(bsz, t, D_ATTN).astype(f32) * jax.nn.silu(z_attn.astype(f32))
    mixed = jnp.concatenate([y_ssm, y_attn], axis=-1).astype(x.dtype) @ w_out
    h = ALPHA * x.astype(f32) + mixed.astype(f32)
    mu = jnp.mean(h, axis=-1, keepdims=True)
    hc = h - mu
    var = jnp.mean(hc * hc, axis=-1, keepdims=True)
    out = (hc * lax.rsqrt(var + LN_EPS) * ln_g.astype(f32) + ln_b.astype(f32)).astype(x.dtype)
    return out, new_conv, new_ssm.astype(ssm_state.dtype), k, v, logf.astype(past_logf.dtype)


def setup_inputs(seed: int = 0) -> dict:
    key = jax.random.key(seed)
    ks = jax.random.split(key, 20)
    f32 = jnp.float32
    x_prompt = jax.random.normal(ks[0], (BATCH, SEQ, D_MODEL), f32)
    x_sample = jax.random.normal(ks[1], (DEC_BATCH, DEC_SEQ, D_MODEL), f32)
    cache_k = jax.random.normal(ks[2], (DEPTH, DEC_BATCH, PAST_LEN, N_ATTN_HEADS, ATTN_HEAD_DIM), f32)
    cache_v = jax.random.normal(ks[3], (DEPTH, DEC_BATCH, PAST_LEN, N_ATTN_HEADS, ATTN_HEAD_DIM), f32)
    cache_logf = jax.nn.log_sigmoid(jax.random.uniform(ks[4], (DEPTH, DEC_BATCH, PAST_LEN, N_ATTN_HEADS), f32, 1.0, 6.0))
    state_conv = jax.random.normal(ks[5], (DEPTH, DEC_BATCH, CONV_W - 1, D_XBC), f32)
    state_ssm = 0.5 * jax.random.normal(ks[6], (DEPTH, DEC_BATCH, N_SSM_HEADS, SSM_HEAD_DIM, D_STATE), f32)
    w_in = jax.random.normal(ks[7], (DEPTH, D_MODEL, D_IN_PROJ), f32) * D_MODEL ** -0.5
    conv_w = jax.random.normal(ks[8], (DEPTH, CONV_W, D_XBC), f32) * CONV_W ** -0.5
    conv_b = 0.01 * jax.random.normal(ks[9], (DEPTH, D_XBC), f32)
    dt0 = jnp.exp(jax.random.uniform(ks[10], (DEPTH, N_SSM_HEADS), f32, math.log(1e-3), math.log(1e-1)))
    dt_bias = dt0 + jnp.log(-jnp.expm1(-dt0))
    a_log = jnp.log(jax.random.uniform(ks[11], (DEPTH, N_SSM_HEADS), f32, 1.0, 16.0))
    d_skip = 1.0 + 0.1 * jax.random.normal(ks[12], (DEPTH, N_SSM_HEADS), f32)
    ssm_norm_w = 1.0 + 0.01 * jax.random.normal(ks[13], (DEPTH, D_SSM), f32)
    f_bias = jax.random.uniform(ks[14], (DEPTH, N_ATTN_HEADS), f32, 1.0, 6.0)
    w_out = jax.random.normal(ks[15], (DEPTH, D_MIX, D_MODEL), f32) * (D_MIX ** -0.5) * BETA
    ln_g = 1.0 + 0.01 * jax.random.normal(ks[16], (DEPTH, D_MODEL), f32)
    ln_b = 0.01 * jax.random.normal(ks[17], (DEPTH, D_MODEL), f32)
    return {'x_prompt': x_prompt, 'x_sample': x_sample, 'cache_k': cache_k, 'cache_v': cache_v,
            'cache_logf': cache_logf, 'state_conv': state_conv, 'state_ssm': state_ssm,
            'w_in': w_in, 'conv_w': conv_w, 'conv_b': conv_b, 'dt_bias': dt_bias, 'a_log': a_log,
            'd_skip': d_skip, 'ssm_norm_w': ssm_norm_w, 'f_bias': f_bias, 'w_out': w_out,
            'ln_g': ln_g, 'ln_b': ln_b}


def reference(x_prompt, x_sample, cache_k, cache_v, cache_logf, state_conv, state_ssm,
              w_in, conv_w, conv_b, dt_bias, a_log, d_skip, ssm_norm_w, f_bias, w_out, ln_g, ln_b):
    yp, ys = x_prompt, x_sample
    kp_l, vp_l, fp_l, cp_l, sp_l = [], [], [], [], []
    ks_l, vs_l, fs_l, cs_l, ss_l = [], [], [], [], []
    bp = x_prompt.shape[0]
    for layer in range(DEPTH):
        params = (w_in[layer], conv_w[layer], conv_b[layer], dt_bias[layer], a_log[layer], d_skip[layer],
                  ssm_norm_w[layer], f_bias[layer], w_out[layer], ln_g[layer], ln_b[layer])
        conv0 = jnp.zeros((bp, CONV_W - 1, D_XBC), yp.dtype)
        ssm0 = jnp.zeros((bp, N_SSM_HEADS, SSM_HEAD_DIM, D_STATE), state_ssm.dtype)
        k0 = jnp.zeros((bp, 0, N_ATTN_HEADS, ATTN_HEAD_DIM), yp.dtype)
        f0 = jnp.zeros((bp, 0, N_ATTN_HEADS), cache_logf.dtype)
        yp, cp, sp, kp, vp, fp = hybrid_layer(yp, conv0, ssm0, k0, k0, f0, *params,
                                              ssm_chunk=CHUNK, q_block=Q_BLOCK)
        ys, cs, ss, kn, vn, fn = hybrid_layer(ys, state_conv[layer], state_ssm[layer], cache_k[layer],
                                              cache_v[layer], cache_logf[layer], *params,
                                              ssm_chunk=x_sample.shape[1], q_block=None)
        kp_l.append(kp); vp_l.append(vp); fp_l.append(fp); cp_l.append(cp); sp_l.append(sp)
        ks_l.append(kn); vs_l.append(vn); fs_l.append(fn); cs_l.append(cs); ss_l.append(ss)
    return (yp, ys,
            jnp.stack(kp_l), jnp.stack(vp_l), jnp.stack(fp_l), jnp.stack(cp_l), jnp.stack(sp_l),
            jnp.stack(ks_l), jnp.stack(vs_l), jnp.stack(fs_l), jnp.stack(cs_l), jnp.stack(ss_l))
```

```python
import functools

import numpy as np
import jax
import jax.numpy as jnp
from jax import lax
from jax.experimental import pallas as pl
from jax.experimental.pallas import tpu as pltpu

F32 = jnp.float32
BF16 = jnp.bfloat16

D_MODEL = 1024
D_SSM = 1024
D_ATTN = 1024
HEAD_DIM = 64
N_HEADS = 16
N_PAIRS = N_HEADS // 2
N_GROUPS = 2
D_STATE = 128
D_BC = N_GROUPS * D_STATE
CONV_W = 4
D_XBC = D_SSM + 2 * D_BC
ATTN_SCALE = HEAD_DIM ** -0.5
LN_EPS = 1e-5
RMS_EPS = 1e-5
LANES = 128
SSD_CHUNK = 128
MASK_VALUE = -1e30
VMEM_LIMIT = 56 * 1024 * 1024

C_ZS, C_XBC, C_Q, C_K, C_V, C_ZA, C_SM = 0, 1024, 2560, 3584, 4608, 5632, 6656
N_PROJ = C_SM + LANES
SPLIT_ONE_LANE = 3 * N_HEADS


def _silu(z):
    return z * jax.nn.sigmoid(z)


def _split3_packed(val, lane, base):
    hi = val.astype(BF16).astype(F32)
    r1 = val - hi
    mid = r1.astype(BF16).astype(F32)
    lo = r1 - mid

    def to(v, dst):
        shift = (dst - base) % LANES
        return v if shift == 0 else pltpu.roll(v, shift, 1)

    return jnp.where(lane < 16, to(hi, 0),
                     jnp.where(lane < 32, to(mid, 16),
                               jnp.where(lane < 48, to(lo, 32), 0.0)))


def _proj_kernel(*refs, tm, chunk, aug):
    if aug:
        (x_ref, w_ref, cst_ref, cw_ref, cb_ref, sbias_ref, alog_ref, pext_ref,
         k_ref, v_ref, qa_ref, ka_ref, v16_ref, ga_ref, gs_ref, xbc_ref, sm_ref, csa_ref,
         sct_ref, smt_ref, cout_ref, xp_sc, fc_sc) = refs
    else:
        (x_ref, w_ref, cst_ref, cw_ref, cb_ref, sbias_ref, alog_ref,
         k_ref, v_ref, q16_ref, ga_ref, gs_ref, xbc_ref, sm_ref, csa_ref,
         sct_ref, smt_ref, cout_ref, xp_sc, fc_sc) = refs
    t = pl.program_id(1)
    nt = pl.num_programs(1)
    xb = x_ref[0].astype(BF16)

    def proj(lo, hi):
        return jnp.dot(xb, w_ref[:, lo:hi], preferred_element_type=F32)

    gs_ref[0] = _silu(proj(C_ZS, C_XBC)).astype(BF16)
    ga_ref[0] = _silu(proj(C_ZA, C_SM)).astype(BF16)
    kk = proj(C_K, C_V)
    k_ref[0] = kk
    vv = proj(C_V, C_ZA)
    v_ref[0] = vv
    qq = proj(C_Q, C_K) * ATTN_SCALE

    lane = lax.broadcasted_iota(jnp.int32, (tm, LANES), 1)
    row = lax.broadcasted_iota(jnp.int32, (tm, LANES), 0)
    u = proj(C_SM, N_PROJ) + sbias_ref[...]
    dt = jax.nn.softplus(u)
    logf = jax.nn.log_sigmoid(u)
    sm = jnp.where(lane < 16, dt, jnp.where(lane < 32, logf, 0.0))
    sm_ref[0] = sm
    aneg = -jnp.exp(alog_ref[...])
    s = jnp.where(lane < 16, dt * aneg, jnp.where(lane < 32, logf, 0.0))
    rowc = row & (chunk - 1)
    sh = 1
    while sh < tm:
        ok = (row >= sh) & ((lane >= 16) | (rowc >= sh))
        s = s + jnp.where(ok, pltpu.roll(s, sh, 0), 0.0)
        sh *= 2

    @pl.when(t == 0)
    def _():
        fc_sc[...] = jnp.zeros_like(fc_sc)

    sc = s + fc_sc[...]
    lane1 = lax.broadcasted_iota(jnp.int32, (1, LANES), 1)
    fc_sc[...] = jnp.where((lane1 >= 16) & (lane1 < 32), sc[tm - 1:tm, :], 0.0)
    sct_ref[0] = sc.T
    smt_ref[0] = sm.T
    csa_ref[0] = _split3_packed(sc, lane, 0).astype(BF16)

    if aug:
        fa = _split3_packed(sc, lane, 16)
        fa = jnp.where(lane == SPLIT_ONE_LANE, 1.0, fa).astype(BF16)
        ext = jnp.dot(fa, pext_ref[...], preferred_element_type=F32).astype(BF16)
        for c in range(N_PAIRS):
            src = slice(LANES * c, LANES * (c + 1))
            qa_ref[0, :, 2 * LANES * c:2 * LANES * c + LANES] = qq[:, src].astype(BF16)
            qa_ref[0, :, 2 * LANES * c + LANES:2 * LANES * (c + 1)] = ext[:, src]
            ka_ref[0, :, 2 * LANES * c:2 * LANES * c + LANES] = kk[:, src].astype(BF16)
            ka_ref[0, :, 2 * LANES * c + LANES:2 * LANES * (c + 1)] = ext[:, D_ATTN + LANES * c:D_ATTN + LANES * (c + 1)]
        v16_ref[0] = vv.astype(BF16)
    else:
        q16_ref[0] = qq.astype(BF16)

    @pl.when(t == 0)
    def _():
        xp_sc[5:8, :] = cst_ref[0]

    @pl.when(t > 0)
    def _():
        xp_sc[5:8, :] = xp_sc[tm + 5:tm + 8, :]

    xp_sc[8:8 + tm, :] = proj(C_XBC, C_Q)
    for j in range(D_XBC // LANES):
        sl = slice(LANES * j, LANES * (j + 1))
        acc = cb_ref[:, sl] + cw_ref[3:4, sl] * xp_sc[8:8 + tm, sl]
        acc = acc + cw_ref[2:3, sl] * xp_sc[7:7 + tm, sl]
        acc = acc + cw_ref[1:2, sl] * xp_sc[6:6 + tm, sl]
        acc = acc + cw_ref[0:1, sl] * xp_sc[5:5 + tm, sl]
        xbc_ref[0, :, sl] = _silu(acc).astype(BF16)

    @pl.when(t == nt - 1)
    def _():
        cout_ref[0] = xp_sc[tm + 5:tm + 8, :]


def _const_spec(shape):
    return pl.BlockSpec(shape, lambda b, t: (0,) * len(shape), pipeline_mode=pl.Buffered(1))


def _proj_call(x, wp, cst, cw, cb, sbias, alog, pext, *, tm, chunk, aug):
    bsz, tlen, _ = x.shape
    assert tlen % tm == 0 and tm % chunk == 0 and chunk & (chunk - 1) == 0
    grid = (bsz, tlen // tm)

    def tok(width):
        return pl.BlockSpec((1, tm, width), lambda b, t: (b, t, 0))

    tr = pl.BlockSpec((1, LANES, tm), lambda b, t: (b, 0, t))
    per_b = pl.BlockSpec((1, CONV_W - 1, D_XBC), lambda b, t: (b, 0, 0))
    in_specs = [tok(D_MODEL), _const_spec((D_MODEL, N_PROJ)), per_b, _const_spec((CONV_W, D_XBC)),
                _const_spec((1, D_XBC)), _const_spec((1, LANES)), _const_spec((1, LANES))]
    args = [x, wp, cst, cw, cb, sbias, alog]

    def sds(width, dtype):
        return jax.ShapeDtypeStruct((bsz, tlen, width), dtype)

    out_shape = [sds(D_ATTN, F32), sds(D_ATTN, F32)]
    out_specs = [tok(D_ATTN), tok(D_ATTN)]
    if aug:
        in_specs.append(_const_spec((LANES, 2 * D_ATTN)))
        args.append(pext)
        out_shape += [sds(2 * D_ATTN, BF16), sds(2 * D_ATTN, BF16), sds(D_ATTN, BF16)]
        out_specs += [tok(2 * D_ATTN), tok(2 * D_ATTN), tok(D_ATTN)]
    else:
        out_shape += [sds(D_ATTN, BF16)]
        out_specs += [tok(D_ATTN)]
    out_shape += [sds(D_ATTN, BF16), sds(D_SSM, BF16), sds(D_XBC, BF16), sds(LANES, F32), sds(LANES, BF16),
                  jax.ShapeDtypeStruct((bsz, LANES, tlen), F32), jax.ShapeDtypeStruct((bsz, LANES, tlen), F32),
                  jax.ShapeDtypeStruct((bsz, CONV_W - 1, D_XBC), F32)]
    out_specs += [tok(D_ATTN), tok(D_SSM), tok(D_XBC), tok(LANES), tok(LANES), tr, tr, per_b]
    return pl.pallas_call(
        functools.partial(_proj_kernel, tm=tm, chunk=chunk, aug=aug),
        grid=grid, in_specs=in_specs, out_specs=out_specs, out_shape=out_shape,
        scratch_shapes=[pltpu.VMEM((tm + 8, D_XBC), F32), pltpu.VMEM((1, LANES), F32)],
        compiler_params=pltpu.CompilerParams(dimension_semantics=("arbitrary", "arbitrary"),
                                             vmem_limit_bytes=VMEM_LIMIT),
        name="proj_aug" if aug else "proj_dec",
    )(*args)


def _ssd_kernel(xbc_ref, gate_ref, csa_ref, sct_ref, smt_ref, h0_ref, prep_ref, pcd_ref, dskip_ref, nw_ref,
                y_ref, hout_ref, h_sc, y_sc, *, nsub):
    L = SSD_CHUNK
    t = pl.program_id(1)
    nt = pl.num_programs(1)

    @pl.when(t == 0)
    def _():
        h_sc[...] = h0_ref[0]

    lane = lax.broadcasted_iota(jnp.int32, (L, LANES), 1)
    low = lane < HEAD_DIM
    tri = lax.broadcasted_iota(jnp.int32, (L, L), 0) >= lax.broadcasted_iota(jnp.int32, (L, L), 1)
    zero16 = jnp.zeros((L, LANES), BF16)

    for j in range(nsub):
        rows = slice(j * L, (j + 1) * L)
        csa = csa_ref[0, rows, :]
        csrep = jnp.dot(csa, prep_ref[...], preferred_element_type=F32)
        cst = sct_ref[0, 0:N_HEADS, rows]
        dtt = smt_ref[0, 0:N_HEADS, rows]
        w2t = dtt * jnp.exp(cst[:, L - 1:L] - cst)
        cd = jnp.exp(jnp.dot(csa[L - 8:L, :], pcd_ref[...], preferred_element_type=F32)[7:8, :])
        for g in range(N_GROUPS):
            bg = xbc_ref[0, rows, D_SSM + D_STATE * g:D_SSM + D_STATE * (g + 1)]
            cg = xbc_ref[0, rows, D_SSM + D_BC + D_STATE * g:D_SSM + D_BC + D_STATE * (g + 1)]
            cb = lax.dot_general(cg, bg, (((1,), (1,)), ((), ())), preferred_element_type=F32)
            bgt = bg.astype(F32).T
            cgf = cg.astype(F32)

            def head_parts(h):
                rep = csrep[:, LANES * h:LANES * (h + 1)]
                dec = jnp.exp(jnp.where(tri, rep - cst[h:h + 1, :], -jnp.inf))
                m = (cb * dec * dtt[h:h + 1, :]).astype(BF16)
                ce = (cgf * jnp.exp(rep)).astype(BF16)
                btw = (bgt * w2t[h:h + 1, :]).astype(BF16)
                return m, ce, btw

            for cc in range(N_PAIRS // N_GROUPS):
                c = (N_PAIRS // N_GROUPS) * g + cc
                cols = slice(LANES * c, LANES * (c + 1))
                xs = xbc_ref[0, rows, cols]
                xa = jnp.where(low, xs, zero16)
                xb = jnp.where(low, zero16, xs)
                hp = h_sc[:, cols]
                hp16 = hp.astype(BF16)
                ha = jnp.where(low, hp16, zero16)
                hb = jnp.where(low, zero16, hp16)
                ma, cea, btwa = head_parts(2 * c)
                mb, ceb, btwb = head_parts(2 * c + 1)
                y = jnp.dot(jnp.concatenate([ma, cea], axis=1), jnp.concatenate([xa, ha], axis=0),
                            preferred_element_type=F32)
                y = y + jnp.dot(jnp.concatenate([mb, ceb], axis=1), jnp.concatenate([xb, hb], axis=0),
                                preferred_element_type=F32)
                st = jnp.dot(jnp.concatenate([btwa, btwb], axis=1), jnp.concatenate([xa, xb], axis=0),
                             preferred_element_type=F32)
                h_sc[:, cols] = hp * cd[:, cols] + st
                y_sc[rows, cols] = y + dskip_ref[:, cols] * xs.astype(F32)

    half = D_SSM // N_GROUPS
    for g in range(N_GROUPS):
        cols = slice(half * g, half * (g + 1))
        yg = y_sc[:, cols] * gate_ref[0, :, cols].astype(F32)
        ms = jnp.mean(yg * yg, axis=-1, keepdims=True)
        y_ref[0, :, cols] = (yg * lax.rsqrt(ms + RMS_EPS) * nw_ref[:, cols]).astype(BF16)

    @pl.when(t == nt - 1)
    def _():
        hout_ref[0] = h_sc[...]


def _ssd_call(xbc, gate, csa, sct, smt, h0t, prep, pcd, dskip, nw, *, nsub):
    bsz, tlen, _ = xbc.shape
    tmc = SSD_CHUNK * nsub
    assert tlen % tmc == 0
    grid = (bsz, tlen // tmc)

    def tok(width):
        return pl.BlockSpec((1, tmc, width), lambda b, t: (b, t, 0))

    tr = pl.BlockSpec((1, LANES, tmc), lambda b, t: (b, 0, t))
    st = pl.BlockSpec((1, D_STATE, D_SSM), lambda b, t: (b, 0, 0))
    return pl.pallas_call(
        functools.partial(_ssd_kernel, nsub=nsub),
        grid=grid,
        in_specs=[tok(D_XBC), tok(D_SSM), tok(LANES), tr, tr, st,
                  _const_spec((LANES, N_HEADS * LANES)), _const_spec((LANES, D_SSM)),
                  _const_spec((1, D_SSM)), _const_spec((1, D_SSM))],
        out_specs=[tok(D_SSM), st],
        out_shape=[jax.ShapeDtypeStruct((bsz, tlen, D_SSM), BF16),
                   jax.ShapeDtypeStruct((bsz, D_STATE, D_SSM), F32)],
        scratch_shapes=[pltpu.VMEM((D_STATE, D_SSM), F32), pltpu.VMEM((tmc, D_SSM), F32)],
        compiler_params=pltpu.CompilerParams(dimension_semantics=("arbitrary", "arbitrary"),
                                             vmem_limit_bytes=VMEM_LIMIT),
        name="ssd",
    )(xbc, gate, csa, sct, smt, h0t, prep, pcd, dskip, nw)


def _fox_kernel(qa_ref, ka_ref, v_ref, g_ref, o_ref, m_sc, l_sc, acc_sc, *, tq):
    i = pl.program_id(2)
    lane2 = lax.broadcasted_iota(jnp.int32, (1, 2 * LANES), 1)
    n_ext = 6
    in_a = (lane2 < HEAD_DIM) | ((lane2 >= LANES) & (lane2 < LANES + n_ext))
    in_b = ((lane2 >= HEAD_DIM) & (lane2 < LANES)) | ((lane2 >= LANES + n_ext) & (lane2 < LANES + 2 * n_ext))
    q2 = qa_ref[0]
    zq = jnp.zeros_like(q2)
    qs = (jnp.where(in_a, q2, zq), jnp.where(in_b, q2, zq))
    m_sc[...] = jnp.full_like(m_sc, MASK_VALUE)
    l_sc[...] = jnp.zeros_like(l_sc)
    acc_sc[...] = jnp.zeros_like(acc_sc)
    causal = lax.broadcasted_iota(jnp.int32, (tq, tq), 1) <= lax.broadcasted_iota(jnp.int32, (tq, tq), 0)

    def step(j, masked):
        start = pl.multiple_of(j * tq, tq)
        kblk = ka_ref[0, pl.ds(start, tq), :]
        vblk = v_ref[0, pl.ds(start, tq), :]
        for x in range(2):
            s = lax.dot_general(qs[x], kblk, (((1,), (1,)), ((), ())), preferred_element_type=F32)
            if masked:
                s = jnp.where(causal, s, MASK_VALUE)
            m_old = m_sc[x]
            m_new = jnp.maximum(m_old, jnp.max(s, axis=-1, keepdims=True))
            p = jnp.exp(s - m_new)
            alpha = jnp.exp(m_old - m_new)
            l_sc[x] = alpha * l_sc[x] + jnp.sum(p, axis=-1, keepdims=True)
            acc_sc[x] = alpha * acc_sc[x] + jnp.dot(p.astype(BF16), vblk, preferred_element_type=F32)
            m_sc[x] = m_new

    def body(j, carry):
        step(j, False)
        return carry

    lax.fori_loop(0, i, body, 0)
    step(i, True)
    lane = lax.broadcasted_iota(jnp.int32, (tq, LANES), 1)
    o = jnp.where(lane < HEAD_DIM, acc_sc[0] / l_sc[0], acc_sc[1] / l_sc[1])
    o_ref[0] = (o * g_ref[0].astype(F32)).astype(BF16)


def _fox_call(qa, ka, v16, ga, *, tq):
    bsz, tlen, _ = v16.shape
    assert tlen % tq == 0
    grid = (bsz, N_PAIRS, tlen // tq)
    return pl.pallas_call(
        functools.partial(_fox_kernel, tq=tq),
        grid=grid,
        in_specs=[pl.BlockSpec((1, tq, 2 * LANES), lambda b, c, i: (b, i, c)),
                  pl.BlockSpec((1, tlen, 2 * LANES), lambda b, c, i: (b, 0, c)),
                  pl.BlockSpec((1, tlen, LANES), lambda b, c, i: (b, 0, c)),
                  pl.BlockSpec((1, tq, LANES), lambda b, c, i: (b, i, c))],
        out_specs=pl.BlockSpec((1, tq, LANES), lambda b, c, i: (b, i, c)),
        out_shape=jax.ShapeDtypeStruct((bsz, tlen, D_ATTN), BF16),
        scratch_shapes=[pltpu.VMEM((2, tq, 1), F32), pltpu.VMEM((2, tq, 1), F32),
                        pltpu.VMEM((2, tq, LANES), F32)],
        compiler_params=pltpu.CompilerParams(dimension_semantics=("arbitrary", "arbitrary", "arbitrary"),
                                             vmem_limit_bytes=VMEM_LIMIT),
        name="fox_prompt",
    )(qa, ka, v16, ga)


def _fox_dec_kernel(q_ref, kc_ref, vc_ref, kn_ref, vn_ref, lft_ref, lfc_ref, g_ref, o_ref, *, past, tn, npad):
    width = past + npad
    fkt = lft_ref[0]
    lane_w = lax.broadcasted_iota(jnp.int32, fkt.shape, 1)
    sh = 1
    while sh < width:
        fkt = fkt + jnp.where(lane_w >= sh, pltpu.roll(fkt, sh, 1), 0.0)
        sh *= 2
    fcol = lfc_ref[0]
    row_w = lax.broadcasted_iota(jnp.int32, fcol.shape, 0)
    sh = 1
    while sh < width:
        fcol = fcol + jnp.where(row_w >= sh, pltpu.roll(fcol, sh, 0), 0.0)
        sh *= 2
    fq = fcol[past:past + tn, :]
    lane = lax.broadcasted_iota(jnp.int32, (tn, LANES), 1)
    low = lane < HEAD_DIM
    causal = lax.broadcasted_iota(jnp.int32, (tn, npad), 1) <= lax.broadcasted_iota(jnp.int32, (tn, npad), 0)
    nt_dims = (((1,), (1,)), ((), ()))
    for c in range(N_PAIRS):
        cols = slice(LANES * c, LANES * (c + 1))
        q2 = q_ref[0, :, cols]
        zq = jnp.zeros_like(q2)
        kc2 = kc_ref[0, :, cols].astype(BF16)
        vc2 = vc_ref[0, :, cols].astype(BF16)
        kn2 = kn_ref[0, :, cols].astype(BF16)
        vn2 = vn_ref[0, :, cols].astype(BF16)
        outs = []
        for x in range(2):
            h = 2 * c + x
            qx = jnp.where(low, q2, zq) if x == 0 else jnp.where(low, zq, q2)
            fqh = fq[:, h:h + 1]
            s_c = lax.dot_general(qx, kc2, nt_dims, preferred_element_type=F32) + fqh - fkt[h:h + 1, 0:past]
            s_n = lax.dot_general(qx, kn2, nt_dims, preferred_element_type=F32) + fqh - fkt[h:h + 1, past:width]
            s_n = jnp.where(causal, s_n, MASK_VALUE)
            m = jnp.maximum(jnp.max(s_c, axis=-1, keepdims=True), jnp.max(s_n, axis=-1, keepdims=True))
            p_c = jnp.exp(s_c - m)
            p_n = jnp.exp(s_n - m)
            den = jnp.sum(p_c, axis=-1, keepdims=True) + jnp.sum(p_n, axis=-1, keepdims=True)
            o = jnp.dot(p_c.astype(BF16), vc2, preferred_element_type=F32)
            o = o + jnp.dot(p_n.astype(BF16), vn2, preferred_element_type=F32)
            outs.append(o / den)
        o2 = jnp.where(low, outs[0], outs[1])
        o_ref[0, :, cols] = (o2 * g_ref[0, :, cols].astype(F32)).astype(BF16)


def _fox_dec_call(q16, kc, vc, kn, vn, lft, lfc, ga, *, tn):
    bsz, past, _ = kc.shape
    npad = kn.shape[1]
    width = past + npad

    def per_b(shape):
        return pl.BlockSpec((1,) + shape, lambda b: (b, 0, 0))

    return pl.pallas_call(
        functools.partial(_fox_dec_kernel, past=past, tn=tn, npad=npad),
        grid=(bsz,),
        in_specs=[per_b((tn, D_ATTN)), per_b((past, D_ATTN)), per_b((past, D_ATTN)),
                  per_b((npad, D_ATTN)), per_b((npad, D_ATTN)),
                  per_b((N_HEADS, width)), per_b((width, LANES)), per_b((tn, D_ATTN))],
        out_specs=per_b((tn, D_ATTN)),
        out_shape=jax.ShapeDtypeStruct((bsz, tn, D_ATTN), BF16),
        compiler_params=pltpu.CompilerParams(dimension_semantics=("arbitrary",),
                                             vmem_limit_bytes=VMEM_LIMIT),
        name="fox_dec",
    )(q16, kc, vc, kn, vn, lft, lfc, ga)


def _out_kernel(ys_ref, ya_ref, x_ref, w_ref, g_ref, b_ref, o_ref, *, alpha):
    mixed = jnp.dot(ys_ref[...], w_ref[0:D_SSM, :], preferred_element_type=F32)
    mixed = mixed + jnp.dot(ya_ref[...], w_ref[D_SSM:D_SSM + D_ATTN, :], preferred_element_type=F32)
    h = alpha * x_ref[...] + mixed
    mu = jnp.mean(h, axis=-1, keepdims=True)
    hc = h - mu
    var = jnp.mean(hc * hc, axis=-1, keepdims=True)
    o_ref[...] = hc * lax.rsqrt(var + LN_EPS) * g_ref[...] + b_ref[...]


def _out_call(ys, ya, x, w16, g, b, *, alpha, tm):
    m = x.shape[0]
    assert m % tm == 0
    row = pl.BlockSpec((tm, D_MODEL), lambda i: (i, 0))

    def const(shape):
        return pl.BlockSpec(shape, lambda i: (0, 0), pipeline_mode=pl.Buffered(1))

    return pl.pallas_call(
        functools.partial(_out_kernel, alpha=alpha),
        grid=(m // tm,),
        in_specs=[row, row, row, const((D_SSM + D_ATTN, D_MODEL)), const((1, D_MODEL)), const((1, D_MODEL))],
        out_specs=row,
        out_shape=jax.ShapeDtypeStruct((m, D_MODEL), F32),
        compiler_params=pltpu.CompilerParams(dimension_semantics=("arbitrary",),
                                             vmem_limit_bytes=VMEM_LIMIT),
        name="out_proj",
    )(ys, ya, x, w16, g, b)


def _placement_constants():
    prep = np.zeros((LANES, N_HEADS * LANES), np.float32)
    pcd = np.zeros((LANES, D_SSM), np.float32)
    pext = np.zeros((LANES, 2 * D_ATTN), np.float32)
    for h in range(N_HEADS):
        for part in range(3):
            prep[16 * part + h, LANES * h:LANES * (h + 1)] = 1.0
            pcd[16 * part + h, HEAD_DIM * h:HEAD_DIM * (h + 1)] = 1.0
        base = LANES * (h // 2) + 6 * (h % 2)
        for part in range(3):
            pext[16 * part + h, base + part] = 1.0
            pext[SPLIT_ONE_LANE, base + 3 + part] = 1.0
            pext[SPLIT_ONE_LANE, D_ATTN + base + part] = 1.0
            pext[16 * part + h, D_ATTN + base + 3 + part] = -1.0
    return jnp.asarray(prep, BF16), jnp.asarray(pcd, BF16), jnp.asarray(pext, BF16)


def _permute_w_in(w):
    z_ssm, xbc = w[:, 0:1024], w[:, 1024:2560]
    dt, q, k, v = w[:, 2560:2576], w[:, 2576:3600], w[:, 3600:4624], w[:, 4624:5648]
    z_attn, f = w[:, 5648:6672], w[:, 6672:6688]
    pad = jnp.zeros((w.shape[0], LANES - 2 * N_HEADS), w.dtype)
    return jnp.concatenate([z_ssm, xbc, q, k, v, z_attn, dt, f, pad], axis=1).astype(BF16)


def _pad_rows(a, n, mode):
    return jnp.pad(a, ((0, 0), (0, n - a.shape[1]), (0, 0)), mode=mode)


def _state_to_kernel(h):
    return jnp.transpose(h, (0, 3, 1, 2)).reshape(h.shape[0], D_STATE, D_SSM)


def _state_from_kernel(ht):
    return jnp.transpose(ht.reshape(ht.shape[0], D_STATE, N_HEADS, HEAD_DIM), (0, 2, 3, 1))


def _layer(x, conv_state, ssm_state, past_k, past_v, past_logf, p, *, depth, prompt):
    bsz, tlen, _ = x.shape
    prep, pcd, pext = _placement_constants()
    alpha = (2 * depth) ** 0.25
    if prompt:
        tm = min(512, tlen)
        (k, v, qa, ka, v16, ga, gs, xbc, sm, csa, sct, smt, cout) = _proj_call(
            x, p["wp"], conv_state, p["cw"], p["cb"], p["sbias"], p["alog"], pext,
            tm=tm, chunk=SSD_CHUNK, aug=True)
        nsub = 2 if tlen % (2 * SSD_CHUNK) == 0 else 1
        y_ssm, ht = _ssd_call(xbc, gs, csa, sct, smt, _state_to_kernel(ssm_state), prep, pcd,
                              p["dskip"], p["nw"], nsub=nsub)
        y_attn = _fox_call(qa, ka, v16, ga, tq=min(512, tlen))
    else:
        assert tlen <= SSD_CHUNK
        (k, v, q16, ga, gs, xbc, sm, csa, sct, smt, cout) = _proj_call(
            x, p["wp"], conv_state, p["cw"], p["cb"], p["sbias"], p["alog"], None,
            tm=tlen, chunk=tlen, aug=False)
        y_ssm, ht = _ssd_call(
            _pad_rows(xbc, SSD_CHUNK, "constant"), _pad_rows(gs, SSD_CHUNK, "constant"),
            _pad_rows(csa, SSD_CHUNK, "edge"),
            jnp.pad(sct, ((0, 0), (0, 0), (0, SSD_CHUNK - tlen)), mode="edge"),
            jnp.pad(smt, ((0, 0), (0, 0), (0, SSD_CHUNK - tlen))),
            _state_to_kernel(ssm_state), prep, pcd, p["dskip"], p["nw"], nsub=1)
        y_ssm = y_ssm[:, :tlen]
        past = past_k.shape[1]
        lf_all = jnp.concatenate([past_logf, sm[:, :, 16:32],
                                  jnp.zeros((bsz, LANES - tlen, N_HEADS), F32)], axis=1)
        lft = jnp.transpose(lf_all, (0, 2, 1))
        lfc = jnp.pad(lf_all, ((0, 0), (0, 0), (0, LANES - N_HEADS)))
        y_attn = _fox_dec_call(q16, past_k.reshape(bsz, past, D_ATTN), past_v.reshape(bsz, past, D_ATTN),
                               _pad_rows(k, LANES, "constant"), _pad_rows(v, LANES, "constant"),
                               lft, lfc, ga, tn=tlen)
    m = bsz * tlen
    y = _out_call(y_ssm.reshape(m, D_SSM), y_attn.reshape(m, D_ATTN), x.reshape(m, D_MODEL),
                  p["wo"], p["ln_g"], p["ln_b"], alpha=alpha, tm=min(512, m))
    return (y.reshape(bsz, tlen, D_MODEL), cout, _state_from_kernel(ht),
            k.reshape(bsz, tlen, N_HEADS, HEAD_DIM), v.reshape(bsz, tlen, N_HEADS, HEAD_DIM),
            sm[:, :, 16:32])


def kernel(x_prompt, x_sample, cache_k, cache_v, cache_logf, state_conv, state_ssm, w_in, conv_w, conv_b,
           dt_bias, a_log, d_skip, ssm_norm_w, f_bias, w_out, ln_g, ln_b):
    depth = w_in.shape[0]
    yp, ys = x_prompt, x_sample
    bp = x_prompt.shape[0]
    outs_p, outs_s = [], []
    for layer in range(depth):
        zpad = jnp.zeros((1, LANES - 2 * N_HEADS), F32)
        p = dict(
            wp=_permute_w_in(w_in[layer]),
            cw=conv_w[layer], cb=conv_b[layer][None, :],
            sbias=jnp.concatenate([dt_bias[layer][None, :], f_bias[layer][None, :], zpad], axis=1),
            alog=jnp.concatenate([a_log[layer][None, :], jnp.zeros((1, LANES - N_HEADS), F32)], axis=1),
            dskip=jnp.repeat(d_skip[layer], HEAD_DIM)[None, :], nw=ssm_norm_w[layer][None, :],
            wo=w_out[layer].astype(BF16), ln_g=ln_g[layer][None, :], ln_b=ln_b[layer][None, :])
        conv0 = jnp.zeros((bp, CONV_W - 1, D_XBC), F32)
        ssm0 = jnp.zeros((bp, N_HEADS, HEAD_DIM, D_STATE), state_ssm.dtype)
        yp, cp, sp, kp, vp, fp = _layer(yp, conv0, ssm0, None, None, None, p, depth=depth, prompt=True)
        ys, cs, ss, kn, vn, fn = _layer(ys, state_conv[layer], state_ssm[layer], cache_k[layer], cache_v[layer],
                                        cache_logf[layer], p, depth=depth, prompt=False)
        outs_p.append((kp, vp, fp, cp, sp))
        outs_s.append((kn, vn, fn, cs, ss))

    def stack(items, idx):
        return jnp.stack([it[idx] for it in items])

    return (yp, ys, stack(outs_p, 0), stack(outs_p, 1), stack(outs_p, 2), stack(outs_p, 3), stack(outs_p, 4),
            stack(outs_s, 0), stack(outs_s, 1), stack(outs_s, 2), stack(outs_s, 3), stack(outs_s, 4))
```

```python
import functools

import numpy as np
import jax
import jax.numpy as jnp
from jax import lax
from jax.experimental import pallas as pl
from jax.experimental.pallas import tpu as pltpu

F32 = jnp.float32
BF16 = jnp.bfloat16

D_MODEL = 1024
D_SSM = 1024
D_ATTN = 1024
HEAD_DIM = 64
N_HEADS = 16
N_PAIRS = N_HEADS // 2
N_GROUPS = 2
D_STATE = 128
D_BC = N_GROUPS * D_STATE
CONV_W = 4
D_XBC = D_SSM + 2 * D_BC
ATTN_SCALE = HEAD_DIM ** -0.5
LN_EPS = 1e-5
RMS_EPS = 1e-5
LANES = 128
SSD_CHUNK = 128
MASK_VALUE = -1e30
LOG2E = 1.4426950408889634
FOX_TK = 256
VMEM_LIMIT = 56 * 1024 * 1024

C_ZS, C_XBC, C_Q, C_K, C_V, C_ZA, C_SM = 0, 1024, 2560, 3584, 4608, 5632, 6656
N_PROJ = C_SM + LANES
SPLIT_ONE_LANE = 3 * N_HEADS


def _silu(z):
    return z * jax.nn.sigmoid(z)


def _split3_packed(val, lane, base):
    hi = val.astype(BF16).astype(F32)
    r1 = val - hi
    mid = r1.astype(BF16).astype(F32)
    lo = r1 - mid

    def to(v, dst):
        shift = (dst - base) % LANES
        return v if shift == 0 else pltpu.roll(v, shift, 1)

    return jnp.where(lane < 16, to(hi, 0),
                     jnp.where(lane < 32, to(mid, 16),
                               jnp.where(lane < 48, to(lo, 32), 0.0)))


def _proj_kernel(*refs, tm, chunk, aug):
    if aug:
        (x_ref, w_ref, cst_ref, cw_ref, cb_ref, sbias_ref, alog_ref, pext_ref,
         k_ref, v_ref, qa_ref, ka_ref, vt_ref, ga_ref, gs_ref, xbc_ref, sm_ref, csa_ref,
         sct_ref, smt_ref, cout_ref, xp_sc, fc_sc) = refs
    else:
        (x_ref, w_ref, cst_ref, cw_ref, cb_ref, sbias_ref, alog_ref,
         k_ref, v_ref, q16_ref, ga_ref, gs_ref, xbc_ref, sm_ref, csa_ref,
         sct_ref, smt_ref, cout_ref, xp_sc, fc_sc) = refs
    t = pl.program_id(1)
    nt = pl.num_programs(1)
    xb = x_ref[0].astype(BF16)

    def proj(lo, hi):
        return jnp.dot(xb, w_ref[:, lo:hi], preferred_element_type=F32)

    gs_ref[0] = _silu(proj(C_ZS, C_XBC)).astype(BF16)
    ga_ref[0] = _silu(proj(C_ZA, C_SM)).astype(BF16)
    kk = proj(C_K, C_V)
    k_ref[0] = kk
    vv = proj(C_V, C_ZA)
    v_ref[0] = vv
    qq = proj(C_Q, C_K) * ATTN_SCALE

    lane = lax.broadcasted_iota(jnp.int32, (tm, LANES), 1)
    row = lax.broadcasted_iota(jnp.int32, (tm, LANES), 0)
    u = proj(C_SM, N_PROJ) + sbias_ref[...]
    dt = jax.nn.softplus(u)
    logf = jax.nn.log_sigmoid(u)
    sm = jnp.where(lane < 16, dt, jnp.where(lane < 32, logf, 0.0))
    sm_ref[0] = sm
    aneg = -jnp.exp(alog_ref[...])
    s = jnp.where(lane < 16, dt * aneg, jnp.where(lane < 32, logf, 0.0))
    rowc = row & (chunk - 1)
    sh = 1
    while sh < tm:
        ok = (row >= sh) & ((lane >= 16) | (rowc >= sh))
        s = s + jnp.where(ok, pltpu.roll(s, sh, 0), 0.0)
        sh *= 2

    @pl.when(t == 0)
    def _():
        fc_sc[...] = jnp.zeros_like(fc_sc)

    sc = s + fc_sc[...]
    lane1 = lax.broadcasted_iota(jnp.int32, (1, LANES), 1)
    fc_sc[...] = jnp.where((lane1 >= 16) & (lane1 < 32), sc[tm - 1:tm, :], 0.0)
    sct_ref[0] = sc.T
    smt_ref[0] = sm.T
    csa_ref[0] = _split3_packed(sc, lane, 0).astype(BF16)

    if aug:
        qq = qq * LOG2E
        fa = _split3_packed(sc * LOG2E, lane, 16)
        fa = jnp.where(lane == SPLIT_ONE_LANE, 1.0, fa).astype(BF16)
        ext = jnp.dot(fa, pext_ref[...], preferred_element_type=F32).astype(BF16)
        for c in range(N_PAIRS):
            src = slice(LANES * c, LANES * (c + 1))
            qa_ref[0, :, 2 * LANES * c:2 * LANES * c + LANES] = qq[:, src].astype(BF16)
            qa_ref[0, :, 2 * LANES * c + LANES:2 * LANES * (c + 1)] = ext[:, src]
            ka_ref[0, :, 2 * LANES * c:2 * LANES * c + LANES] = kk[:, src].astype(BF16)
            ka_ref[0, :, 2 * LANES * c + LANES:2 * LANES * (c + 1)] = ext[:, D_ATTN + LANES * c:D_ATTN + LANES * (c + 1)]
        for r in range(tm // FOX_TK):
            vt_ref[0, r] = vv[FOX_TK * r:FOX_TK * (r + 1), :].T.astype(BF16)
    else:
        q16_ref[0] = qq.astype(BF16)

    @pl.when(t == 0)
    def _():
        xp_sc[5:8, :] = cst_ref[0]

    @pl.when(t > 0)
    def _():
        xp_sc[5:8, :] = xp_sc[tm + 5:tm + 8, :]

    xp_sc[8:8 + tm, :] = proj(C_XBC, C_Q)
    for j in range(D_XBC // LANES):
        sl = slice(LANES * j, LANES * (j + 1))
        acc = cb_ref[:, sl] + cw_ref[3:4, sl] * xp_sc[8:8 + tm, sl]
        acc = acc + cw_ref[2:3, sl] * xp_sc[7:7 + tm, sl]
        acc = acc + cw_ref[1:2, sl] * xp_sc[6:6 + tm, sl]
        acc = acc + cw_ref[0:1, sl] * xp_sc[5:5 + tm, sl]
        xbc_ref[0, :, sl] = _silu(acc).astype(BF16)

    @pl.when(t == nt - 1)
    def _():
        cout_ref[0] = xp_sc[tm + 5:tm + 8, :]


def _const_spec(shape):
    return pl.BlockSpec(shape, lambda b, t: (0,) * len(shape), pipeline_mode=pl.Buffered(1))


def _proj_call(x, wp, cst, cw, cb, sbias, alog, pext, *, tm, chunk, aug):
    bsz, tlen, _ = x.shape
    assert tlen % tm == 0 and tm % chunk == 0 and chunk & (chunk - 1) == 0
    grid = (bsz, tlen // tm)

    def tok(width):
        return pl.BlockSpec((1, tm, width), lambda b, t: (b, t, 0))

    tr = pl.BlockSpec((1, LANES, tm), lambda b, t: (b, 0, t))
    per_b = pl.BlockSpec((1, CONV_W - 1, D_XBC), lambda b, t: (b, 0, 0))
    in_specs = [tok(D_MODEL), _const_spec((D_MODEL, N_PROJ)), per_b, _const_spec((CONV_W, D_XBC)),
                _const_spec((1, D_XBC)), _const_spec((1, LANES)), _const_spec((1, LANES))]
    args = [x, wp, cst, cw, cb, sbias, alog]

    def sds(width, dtype):
        return jax.ShapeDtypeStruct((bsz, tlen, width), dtype)

    out_shape = [sds(D_ATTN, F32), sds(D_ATTN, F32)]
    out_specs = [tok(D_ATTN), tok(D_ATTN)]
    if aug:
        in_specs.append(_const_spec((LANES, 2 * D_ATTN)))
        args.append(pext)
        assert tm % FOX_TK == 0
        out_shape += [sds(2 * D_ATTN, BF16), sds(2 * D_ATTN, BF16),
                      jax.ShapeDtypeStruct((bsz, tlen // FOX_TK, D_ATTN, FOX_TK), BF16)]
        out_specs += [tok(2 * D_ATTN), tok(2 * D_ATTN),
                      pl.BlockSpec((1, tm // FOX_TK, D_ATTN, FOX_TK), lambda b, t: (b, t, 0, 0))]
    else:
        out_shape += [sds(D_ATTN, BF16)]
        out_specs += [tok(D_ATTN)]
    out_shape += [sds(D_ATTN, BF16), sds(D_SSM, BF16), sds(D_XBC, BF16), sds(LANES, F32), sds(LANES, BF16),
                  jax.ShapeDtypeStruct((bsz, LANES, tlen), F32), jax.ShapeDtypeStruct((bsz, LANES, tlen), F32),
                  jax.ShapeDtypeStruct((bsz, CONV_W - 1, D_XBC), F32)]
    out_specs += [tok(D_ATTN), tok(D_SSM), tok(D_XBC), tok(LANES), tok(LANES), tr, tr, per_b]
    return pl.pallas_call(
        functools.partial(_proj_kernel, tm=tm, chunk=chunk, aug=aug),
        grid=grid, in_specs=in_specs, out_specs=out_specs, out_shape=out_shape,
        scratch_shapes=[pltpu.VMEM((tm + 8, D_XBC), F32), pltpu.VMEM((1, LANES), F32)],
        compiler_params=pltpu.CompilerParams(dimension_semantics=("arbitrary", "arbitrary"),
                                             vmem_limit_bytes=VMEM_LIMIT),
        name="proj_aug" if aug else "proj_dec",
    )(*args)


def _ssd_kernel(xbc_ref, gate_ref, csa_ref, sct_ref, smt_ref, h0_ref, prep_ref, pcd_ref, dskip_ref, nw_ref,
                y_ref, hout_ref, h_sc, y_sc, *, nsub):
    L = SSD_CHUNK
    t = pl.program_id(1)
    nt = pl.num_programs(1)

    @pl.when(t == 0)
    def _():
        h_sc[...] = h0_ref[0]

    lane = lax.broadcasted_iota(jnp.int32, (L, LANES), 1)
    low = lane < HEAD_DIM
    tri = lax.broadcasted_iota(jnp.int32, (L, L), 0) >= lax.broadcasted_iota(jnp.int32, (L, L), 1)
    zero16 = jnp.zeros((L, LANES), BF16)

    for j in range(nsub):
        rows = slice(j * L, (j + 1) * L)
        csa = csa_ref[0, rows, :]
        csrep = jnp.dot(csa, prep_ref[...], preferred_element_type=F32)
        cst = sct_ref[0, 0:N_HEADS, rows]
        dtt = smt_ref[0, 0:N_HEADS, rows]
        w2t = dtt * jnp.exp(cst[:, L - 1:L] - cst)
        cd = jnp.exp(jnp.dot(csa[L - 8:L, :], pcd_ref[...], preferred_element_type=F32)[7:8, :])
        for g in range(N_GROUPS):
            bg = xbc_ref[0, rows, D_SSM + D_STATE * g:D_SSM + D_STATE * (g + 1)]
            cg = xbc_ref[0, rows, D_SSM + D_BC + D_STATE * g:D_SSM + D_BC + D_STATE * (g + 1)]
            cb = lax.dot_general(cg, bg, (((1,), (1,)), ((), ())), preferred_element_type=F32)
            bgt = bg.astype(F32).T
            cgf = cg.astype(F32)

            def head_parts(h):
                rep = csrep[:, LANES * h:LANES * (h + 1)]
                dec = jnp.exp(jnp.where(tri, rep - cst[h:h + 1, :], -jnp.inf))
                m = (cb * dec * dtt[h:h + 1, :]).astype(BF16)
                ce = (cgf * jnp.exp(rep)).astype(BF16)
                btw = (bgt * w2t[h:h + 1, :]).astype(BF16)
                return m, ce, btw

            for cc in range(N_PAIRS // N_GROUPS):
                c = (N_PAIRS // N_GROUPS) * g + cc
                cols = slice(LANES * c, LANES * (c + 1))
                xs = xbc_ref[0, rows, cols]
                xa = jnp.where(low, xs, zero16)
                xb = jnp.where(low, zero16, xs)
                hp = h_sc[:, cols]
                hp16 = hp.astype(BF16)
                ha = jnp.where(low, hp16, zero16)
                hb = jnp.where(low, zero16, hp16)
                ma, cea, btwa = head_parts(2 * c)
                mb, ceb, btwb = head_parts(2 * c + 1)
                y = jnp.dot(jnp.concatenate([ma, cea], axis=1), jnp.concatenate([xa, ha], axis=0),
                            preferred_element_type=F32)
                y = y + jnp.dot(jnp.concatenate([mb, ceb], axis=1), jnp.concatenate([xb, hb], axis=0),
                                preferred_element_type=F32)
                st = jnp.dot(jnp.concatenate([btwa, btwb], axis=1), jnp.concatenate([xa, xb], axis=0),
                             preferred_element_type=F32)
                h_sc[:, cols] = hp * cd[:, cols] + st
                y_sc[rows, cols] = y + dskip_ref[:, cols] * xs.astype(F32)

    half = D_SSM // N_GROUPS
    for g in range(N_GROUPS):
        cols = slice(half * g, half * (g + 1))
        yg = y_sc[:, cols] * gate_ref[0, :, cols].astype(F32)
        ms = jnp.mean(yg * yg, axis=-1, keepdims=True)
        y_ref[0, :, cols] = (yg * lax.rsqrt(ms + RMS_EPS) * nw_ref[:, cols]).astype(BF16)

    @pl.when(t == nt - 1)
    def _():
        hout_ref[0] = h_sc[...]


def _ssd_call(xbc, gate, csa, sct, smt, h0t, prep, pcd, dskip, nw, *, nsub):
    bsz, tlen, _ = xbc.shape
    tmc = SSD_CHUNK * nsub
    assert tlen % tmc == 0
    grid = (bsz, tlen // tmc)

    def tok(width):
        return pl.BlockSpec((1, tmc, width), lambda b, t: (b, t, 0))

    tr = pl.BlockSpec((1, LANES, tmc), lambda b, t: (b, 0, t))
    st = pl.BlockSpec((1, D_STATE, D_SSM), lambda b, t: (b, 0, 0))
    return pl.pallas_call(
        functools.partial(_ssd_kernel, nsub=nsub),
        grid=grid,
        in_specs=[tok(D_XBC), tok(D_SSM), tok(LANES), tr, tr, st,
                  _const_spec((LANES, N_HEADS * LANES)), _const_spec((LANES, D_SSM)),
                  _const_spec((1, D_SSM)), _const_spec((1, D_SSM))],
        out_specs=[tok(D_SSM), st],
        out_shape=[jax.ShapeDtypeStruct((bsz, tlen, D_SSM), BF16),
                   jax.ShapeDtypeStruct((bsz, D_STATE, D_SSM), F32)],
        scratch_shapes=[pltpu.VMEM((D_STATE, D_SSM), F32), pltpu.VMEM((tmc, D_SSM), F32)],
        compiler_params=pltpu.CompilerParams(dimension_semantics=("arbitrary", "arbitrary"),
                                             vmem_limit_bytes=VMEM_LIMIT),
        name="ssd",
    )(xbc, gate, csa, sct, smt, h0t, prep, pcd, dskip, nw)


def _fox_kernel(qa_ref, ka_ref, vt_ref, g_ref, o_ref, qm_sc, m_sc, l_sc, acc_sc, st_sc, *, tq, kb, ahead):
    tk = FOX_TK
    i = pl.program_id(2)
    nslab = tq // tk
    assert nslab % kb == 0
    lane2 = lax.broadcasted_iota(jnp.int32, (1, 2 * LANES), 1)
    n_ext = 6
    in_a = (lane2 < HEAD_DIM) | ((lane2 >= LANES) & (lane2 < LANES + n_ext))
    in_b = ((lane2 >= HEAD_DIM) & (lane2 < LANES)) | ((lane2 >= LANES + n_ext) & (lane2 < LANES + 2 * n_ext))
    q2 = qa_ref[0]
    zq = jnp.zeros_like(q2)
    qm_sc[0] = jnp.where(in_a, q2, zq)
    qm_sc[1] = jnp.where(in_b, q2, zq)
    m_sc[...] = jnp.full_like(m_sc, MASK_VALUE)
    l_sc[...] = jnp.zeros_like(l_sc)
    acc_sc[...] = jnp.zeros_like(acc_sc)
    visible = lax.broadcasted_iota(jnp.int32, (tk, tk), 0) <= lax.broadcasted_iota(jnp.int32, (tk, tk), 1)
    nt_dims = (((1,), (1,)), ((), ()))

    def scores(unit):
        j, x, slab, _ = unit
        kblk = ka_ref[0, pl.ds(pl.multiple_of(j * tk, tk), tk), :]
        return lax.dot_general(kblk, qm_sc[x, tk * slab:tk * (slab + 1), :], nt_dims,
                               preferred_element_type=F32)

    def finish(unit, st):
        j, x, slab, masked = unit
        cols = slice(tk * slab, tk * (slab + 1))
        if masked:
            st = jnp.where(visible, st, MASK_VALUE)
        m_old = m_sc[x, :, cols]
        m_new = jnp.maximum(m_old, jnp.max(st, axis=0, keepdims=True))
        p = jnp.exp2(st - m_new)
        alpha = jnp.exp2(m_old - m_new)
        l_sc[x, :, cols] = alpha * l_sc[x, :, cols] + jnp.sum(p, axis=0, keepdims=True)
        pv = jnp.dot(vt_ref[0, j, HEAD_DIM * x:HEAD_DIM * (x + 1), :], p.astype(BF16),
                     preferred_element_type=F32)
        acc_sc[x, :, cols] = alpha * acc_sc[x, :, cols] + pv
        m_sc[x, :, cols] = m_new

    def run(units, next_units):
        pending = []
        for n, unit in enumerate(units):
            st = st_sc[n] if n < ahead else pending.pop(0)
            if n + ahead < len(units):
                pending.append(scores(units[n + ahead]))
            elif next_units is not None:
                st_sc[n + ahead - len(units)] = scores(next_units[n + ahead - len(units)])
            finish(unit, st)

    def full_units(jo):
        return [(kb * jo + jb, x, slab, False) for jb in range(kb) for x in range(2) for slab in range(nslab)]

    tail_units = [(nslab * i + jj, x, slab, slab == jj)
                  for jj in range(nslab) for x in range(2) for slab in range(jj, nslab)]
    assert all(f[1:3] == t[1:3] for f, t in zip(full_units(0)[:ahead], tail_units[:ahead]))
    assert ahead <= 2 * nslab
    for a, unit in enumerate(full_units(0)[:ahead]):
        st_sc[a] = scores(unit)

    def body(jo, carry):
        run(full_units(jo), full_units(jo + 1))
        return carry

    lax.fori_loop(0, (nslab * i) // kb, body, 0)
    run(tail_units, None)
    ot = jnp.concatenate([acc_sc[0] / l_sc[0], acc_sc[1] / l_sc[1]], axis=0)
    o_ref[0] = (ot.T * g_ref[0].astype(F32)).astype(BF16)


def _fox_call(qa, ka, vt, ga, *, tq, kb, ahead):
    bsz, nkv, _, tk = vt.shape
    tlen = nkv * tk
    assert tlen % tq == 0 and tq % tk == 0 and tk == FOX_TK
    grid = (bsz, N_PAIRS, tlen // tq)
    return pl.pallas_call(
        functools.partial(_fox_kernel, tq=tq, kb=kb, ahead=ahead),
        grid=grid,
        in_specs=[pl.BlockSpec((1, tq, 2 * LANES), lambda b, c, i: (b, i, c)),
                  pl.BlockSpec((1, tlen, 2 * LANES), lambda b, c, i: (b, 0, c)),
                  pl.BlockSpec((1, nkv, LANES, tk), lambda b, c, i: (b, 0, c, 0)),
                  pl.BlockSpec((1, tq, LANES), lambda b, c, i: (b, i, c))],
        out_specs=pl.BlockSpec((1, tq, LANES), lambda b, c, i: (b, i, c)),
        out_shape=jax.ShapeDtypeStruct((bsz, tlen, D_ATTN), BF16),
        scratch_shapes=[pltpu.VMEM((2, tq, 2 * LANES), BF16),
                        pltpu.VMEM((2, 1, tq), F32), pltpu.VMEM((2, 1, tq), F32),
                        pltpu.VMEM((2, HEAD_DIM, tq), F32), pltpu.VMEM((ahead, tk, tk), F32)],
        compiler_params=pltpu.CompilerParams(dimension_semantics=("arbitrary", "arbitrary", "arbitrary"),
                                             vmem_limit_bytes=VMEM_LIMIT),
        name="fox_prompt",
    )(qa, ka, vt, ga)


def _fox_dec_kernel(q_ref, kc_ref, vc_ref, kn_ref, vn_ref, lft_ref, lfc_ref, g_ref, o_ref, *, past, tn, npad):
    width = past + npad
    fkt = lft_ref[0]
    lane_w = lax.broadcasted_iota(jnp.int32, fkt.shape, 1)
    sh = 1
    while sh < width:
        fkt = fkt + jnp.where(lane_w >= sh, pltpu.roll(fkt, sh, 1), 0.0)
        sh *= 2
    fcol = lfc_ref[0]
    row_w = lax.broadcasted_iota(jnp.int32, fcol.shape, 0)
    sh = 1
    while sh < width:
        fcol = fcol + jnp.where(row_w >= sh, pltpu.roll(fcol, sh, 0), 0.0)
        sh *= 2
    fq = fcol[past:past + tn, :]
    lane = lax.broadcasted_iota(jnp.int32, (tn, LANES), 1)
    low = lane < HEAD_DIM
    causal = lax.broadcasted_iota(jnp.int32, (tn, npad), 1) <= lax.broadcasted_iota(jnp.int32, (tn, npad), 0)
    nt_dims = (((1,), (1,)), ((), ()))
    for c in range(N_PAIRS):
        cols = slice(LANES * c, LANES * (c + 1))
        q2 = q_ref[0, :, cols]
        zq = jnp.zeros_like(q2)
        kc2 = kc_ref[0, :, cols].astype(BF16)
        vc2 = vc_ref[0, :, cols].astype(BF16)
        kn2 = kn_ref[0, :, cols].astype(BF16)
        vn2 = vn_ref[0, :, cols].astype(BF16)
        outs = []
        for x in range(2):
            h = 2 * c + x
            qx = jnp.where(low, q2, zq) if x == 0 else jnp.where(low, zq, q2)
            fqh = fq[:, h:h + 1]
            s_c = lax.dot_general(qx, kc2, nt_dims, preferred_element_type=F32) + fqh - fkt[h:h + 1, 0:past]
            s_n = lax.dot_general(qx, kn2, nt_dims, preferred_element_type=F32) + fqh - fkt[h:h + 1, past:width]
            s_n = jnp.where(causal, s_n, MASK_VALUE)
            m = jnp.maximum(jnp.max(s_c, axis=-1, keepdims=True), jnp.max(s_n, axis=-1, keepdims=True))
            p_c = jnp.exp(s_c - m)
            p_n = jnp.exp(s_n - m)
            den = jnp.sum(p_c, axis=-1, keepdims=True) + jnp.sum(p_n, axis=-1, keepdims=True)
            o = jnp.dot(p_c.astype(BF16), vc2, preferred_element_type=F32)
            o = o + jnp.dot(p_n.astype(BF16), vn2, preferred_element_type=F32)
            outs.append(o / den)
        o2 = jnp.where(low, outs[0], outs[1])
        o_ref[0, :, cols] = (o2 * g_ref[0, :, cols].astype(F32)).astype(BF16)


def _fox_dec_call(q16, kc, vc, kn, vn, lft, lfc, ga, *, tn):
    bsz, past, _ = kc.shape
    npad = kn.shape[1]
    width = past + npad

    def per_b(shape):
        return pl.BlockSpec((1,) + shape, lambda b: (b, 0, 0))

    return pl.pallas_call(
        functools.partial(_fox_dec_kernel, past=past, tn=tn, npad=npad),
        grid=(bsz,),
        in_specs=[per_b((tn, D_ATTN)), per_b((past, D_ATTN)), per_b((past, D_ATTN)),
                  per_b((npad, D_ATTN)), per_b((npad, D_ATTN)),
                  per_b((N_HEADS, width)), per_b((width, LANES)), per_b((tn, D_ATTN))],
        out_specs=per_b((tn, D_ATTN)),
        out_shape=jax.ShapeDtypeStruct((bsz, tn, D_ATTN), BF16),
        compiler_params=pltpu.CompilerParams(dimension_semantics=("arbitrary",),
                                             vmem_limit_bytes=VMEM_LIMIT),
        name="fox_dec",
    )(q16, kc, vc, kn, vn, lft, lfc, ga)


def _out_kernel(ys_ref, ya_ref, x_ref, w_ref, g_ref, b_ref, o_ref, *, alpha):
    mixed = jnp.dot(ys_ref[...], w_ref[0:D_SSM, :], preferred_element_type=F32)
    mixed = mixed + jnp.dot(ya_ref[...], w_ref[D_SSM:D_SSM + D_ATTN, :], preferred_element_type=F32)
    h = alpha * x_ref[...] + mixed
    mu = jnp.mean(h, axis=-1, keepdims=True)
    hc = h - mu
    var = jnp.mean(hc * hc, axis=-1, keepdims=True)
    o_ref[...] = hc * lax.rsqrt(var + LN_EPS) * g_ref[...] + b_ref[...]


def _out_call(ys, ya, x, w16, g, b, *, alpha, tm):
    m = x.shape[0]
    assert m % tm == 0
    row = pl.BlockSpec((tm, D_MODEL), lambda i: (i, 0))

    def const(shape):
        return pl.BlockSpec(shape, lambda i: (0, 0), pipeline_mode=pl.Buffered(1))

    return pl.pallas_call(
        functools.partial(_out_kernel, alpha=alpha),
        grid=(m // tm,),
        in_specs=[row, row, row, const((D_SSM + D_ATTN, D_MODEL)), const((1, D_MODEL)), const((1, D_MODEL))],
        out_specs=row,
        out_shape=jax.ShapeDtypeStruct((m, D_MODEL), F32),
        compiler_params=pltpu.CompilerParams(dimension_semantics=("arbitrary",),
                                             vmem_limit_bytes=VMEM_LIMIT),
        name="out_proj",
    )(ys, ya, x, w16, g, b)


def _placement_constants():
    prep = np.zeros((LANES, N_HEADS * LANES), np.float32)
    pcd = np.zeros((LANES, D_SSM), np.float32)
    pext = np.zeros((LANES, 2 * D_ATTN), np.float32)
    for h in range(N_HEADS):
        for part in range(3):
            prep[16 * part + h, LANES * h:LANES * (h + 1)] = 1.0
            pcd[16 * part + h, HEAD_DIM * h:HEAD_DIM * (h + 1)] = 1.0
        base = LANES * (h // 2) + 6 * (h % 2)
        for part in range(3):
            pext[16 * part + h, base + part] = 1.0
            pext[SPLIT_ONE_LANE, base + 3 + part] = 1.0
            pext[SPLIT_ONE_LANE, D_ATTN + base + part] = 1.0
            pext[16 * part + h, D_ATTN + base + 3 + part] = -1.0
    return jnp.asarray(prep, BF16), jnp.asarray(pcd, BF16), jnp.asarray(pext, BF16)


def _permute_w_in(w):
    z_ssm, xbc = w[:, 0:1024], w[:, 1024:2560]
    dt, q, k, v = w[:, 2560:2576], w[:, 2576:3600], w[:, 3600:4624], w[:, 4624:5648]
    z_attn, f = w[:, 5648:6672], w[:, 6672:6688]
    pad = jnp.zeros((w.shape[0], LANES - 2 * N_HEADS), w.dtype)
    return jnp.concatenate([z_ssm, xbc, q, k, v, z_attn, dt, f, pad], axis=1).astype(BF16)


def _pad_rows(a, n, mode):
    return jnp.pad(a, ((0, 0), (0, n - a.shape[1]), (0, 0)), mode=mode)


def _state_to_kernel(h):
    return jnp.transpose(h, (0, 3, 1, 2)).reshape(h.shape[0], D_STATE, D_SSM)


def _state_from_kernel(ht):
    return jnp.transpose(ht.reshape(ht.shape[0], D_STATE, N_HEADS, HEAD_DIM), (0, 2, 3, 1))


def _layer(x, conv_state, ssm_state, past_k, past_v, past_logf, p, *, depth, prompt):
    bsz, tlen, _ = x.shape
    prep, pcd, pext = _placement_constants()
    alpha = (2 * depth) ** 0.25
    if prompt:
        tm = min(512, tlen)
        (k, v, qa, ka, vt, ga, gs, xbc, sm, csa, sct, smt, cout) = _proj_call(
            x, p["wp"], conv_state, p["cw"], p["cb"], p["sbias"], p["alog"], pext,
            tm=tm, chunk=SSD_CHUNK, aug=True)
        nsub = 2 if tlen % (2 * SSD_CHUNK) == 0 else 1
        y_ssm, ht = _ssd_call(xbc, gs, csa, sct, smt, _state_to_kernel(ssm_state), prep, pcd,
                              p["dskip"], p["nw"], nsub=nsub)
        y_attn = _fox_call(qa, ka, vt, ga, tq=min(1024, tlen), kb=2, ahead=3)
    else:
        assert tlen <= SSD_CHUNK
        (k, v, q16, ga, gs, xbc, sm, csa, sct, smt, cout) = _proj_call(
            x, p["wp"], conv_state, p["cw"], p["cb"], p["sbias"], p["alog"], None,
            tm=tlen, chunk=tlen, aug=False)
        y_ssm, ht = _ssd_call(
            _pad_rows(xbc, SSD_CHUNK, "constant"), _pad_rows(gs, SSD_CHUNK, "constant"),
            _pad_rows(csa, SSD_CHUNK, "edge"),
            jnp.pad(sct, ((0, 0), (0, 0), (0, SSD_CHUNK - tlen)), mode="edge"),
            jnp.pad(smt, ((0, 0), (0, 0), (0, SSD_CHUNK - tlen))),
            _state_to_kernel(ssm_state), prep, pcd, p["dskip"], p["nw"], nsub=1)
        y_ssm = y_ssm[:, :tlen]
        past = past_k.shape[1]
        lf_all = jnp.concatenate([past_logf, sm[:, :, 16:32],
                                  jnp.zeros((bsz, LANES - tlen, N_HEADS), F32)], axis=1)
        lft = jnp.transpose(lf_all, (0, 2, 1))
        lfc = jnp.pad(lf_all, ((0, 0), (0, 0), (0, LANES - N_HEADS)))
        y_attn = _fox_dec_call(q16, past_k.reshape(bsz, past, D_ATTN), past_v.reshape(bsz, past, D_ATTN),
                               _pad_rows(k, LANES, "constant"), _pad_rows(v, LANES, "constant"),
                               lft, lfc, ga, tn=tlen)
    m = bsz * tlen
    y = _out_call(y_ssm.reshape(m, D_SSM), y_attn.reshape(m, D_ATTN), x.reshape(m, D_MODEL),
                  p["wo"], p["ln_g"], p["ln_b"], alpha=alpha, tm=min(512, m))
    return (y.reshape(bsz, tlen, D_MODEL), cout, _state_from_kernel(ht),
            k.reshape(bsz, tlen, N_HEADS, HEAD_DIM), v.reshape(bsz, tlen, N_HEADS, HEAD_DIM),
            sm[:, :, 16:32])


def kernel(x_prompt, x_sample, cache_k, cache_v, cache_logf, state_conv, state_ssm, w_in, conv_w, conv_b,
           dt_bias, a_log, d_skip, ssm_norm_w, f_bias, w_out, ln_g, ln_b):
    depth = w_in.shape[0]
    yp, ys = x_prompt, x_sample
    bp = x_prompt.shape[0]
    outs_p, outs_s = [], []
    for layer in range(depth):
        zpad = jnp.zeros((1, LANES - 2 * N_HEADS), F32)
        p = dict(
            wp=_permute_w_in(w_in[layer]),
            cw=conv_w[layer], cb=conv_b[layer][None, :],
            sbias=jnp.concatenate([dt_bias[layer][None, :], f_bias[layer][None, :], zpad], axis=1),
            alog=jnp.concatenate([a_log[layer][None, :], jnp.zeros((1, LANES - N_HEADS), F32)], axis=1),
            dskip=jnp.repeat(d_skip[layer], HEAD_DIM)[None, :], nw=ssm_norm_w[layer][None, :],
            wo=w_out[layer].astype(BF16), ln_g=ln_g[layer][None, :], ln_b=ln_b[layer][None, :])
        conv0 = jnp.zeros((bp, CONV_W - 1, D_XBC), F32)
        ssm0 = jnp.zeros((bp, N_HEADS, HEAD_DIM, D_STATE), state_ssm.dtype)
        yp, cp, sp, kp, vp, fp = _layer(yp, conv0, ssm0, None, None, None, p, depth=depth, prompt=True)
        ys, cs, ss, kn, vn, fn = _layer(ys, state_conv[layer], state_ssm[layer], cache_k[layer], cache_v[layer],
                                        cache_logf[layer], p, depth=depth, prompt=False)
        outs_p.append((kp, vp, fp, cp, sp))
        outs_s.append((kn, vn, fn, cs, ss))

    def stack(items, idx):
        return jnp.stack([it[idx] for it in items])

    return (yp, ys, stack(outs_p, 0), stack(outs_p, 1), stack(outs_p, 2), stack(outs_p, 3), stack(outs_p, 4),
            stack(outs_s, 0), stack(outs_s, 1), stack(outs_s, 2), stack(outs_s, 3), stack(outs_s, 4))
```

```python
import functools

import numpy as np
import jax
import jax.numpy as jnp
from jax import lax
from jax.experimental import pallas as pl
from jax.experimental.pallas import tpu as pltpu

F32 = jnp.float32
BF16 = jnp.bfloat16

D_MODEL = 1024
D_SSM = 1024
D_ATTN = 1024
HEAD_DIM = 64
N_HEADS = 16
N_PAIRS = N_HEADS // 2
N_GROUPS = 2
D_STATE = 128
D_BC = N_GROUPS * D_STATE
CONV_W = 4
D_XBC = D_SSM + 2 * D_BC
ATTN_SCALE = HEAD_DIM ** -0.5
LN_EPS = 1e-5
RMS_EPS = 1e-5
LANES = 128
SSD_CHUNK = 128
MASK_VALUE = -1e30
LOG2E = 1.4426950408889634
FOX_TK = 256
VMEM_LIMIT = 56 * 1024 * 1024

C_ZS, C_XBC, C_Q, C_K, C_V, C_ZA, C_SM = 0, 1024, 2560, 3584, 4608, 5632, 6656
N_PROJ = C_SM + LANES
SPLIT_ONE_LANE = 3 * N_HEADS


def _silu(z):
    return z * jax.nn.sigmoid(z)


def _split3_packed(val, lane, base):
    hi = val.astype(BF16).astype(F32)
    r1 = val - hi
    mid = r1.astype(BF16).astype(F32)
    lo = r1 - mid

    def to(v, dst):
        shift = (dst - base) % LANES
        return v if shift == 0 else pltpu.roll(v, shift, 1)

    return jnp.where(lane < 16, to(hi, 0),
                     jnp.where(lane < 32, to(mid, 16),
                               jnp.where(lane < 48, to(lo, 32), 0.0)))


def _proj_kernel(*refs, tm, chunk, aug):
    if aug:
        (x_ref, w_ref, cst_ref, cw_ref, cb_ref, sbias_ref, alog_ref, pext_ref,
         k_ref, v_ref, qa_ref, ka_ref, vt_ref, ga_ref, gs_ref, xbc_ref, sm_ref, csa_ref,
         sct_ref, smt_ref, cout_ref, xp_sc, fc_sc) = refs
    else:
        (x_ref, w_ref, cst_ref, cw_ref, cb_ref, sbias_ref, alog_ref,
         k_ref, v_ref, q16_ref, ga_ref, gs_ref, xbc_ref, sm_ref, csa_ref,
         sct_ref, smt_ref, cout_ref, xp_sc, fc_sc) = refs
    t = pl.program_id(1)
    nt = pl.num_programs(1)
    width = 2 * LANES

    @pl.when(t == 0)
    def _():
        fc_sc[...] = jnp.zeros_like(fc_sc)
        xp_sc[5:8, :] = cst_ref[0]

    @pl.when(t > 0)
    def _():
        xp_sc[5:8, :] = xp_sc[tm + 5:tm + 8, :]

    xb = x_ref[0].astype(BF16)
    lane = lax.broadcasted_iota(jnp.int32, (tm, LANES), 1)
    state = {}

    def post_small(res):
        row = lax.broadcasted_iota(jnp.int32, (tm, LANES), 0)
        u = res + sbias_ref[...]
        dt = jax.nn.softplus(u)
        logf = jax.nn.log_sigmoid(u)
        sm = jnp.where(lane < 16, dt, jnp.where(lane < 32, logf, 0.0))
        sm_ref[0] = sm
        aneg = -jnp.exp(alog_ref[...])
        s = jnp.where(lane < 16, dt * aneg, jnp.where(lane < 32, logf, 0.0))
        rowc = row & (chunk - 1)
        sh = 1
        while sh < tm:
            ok = (row >= sh) & ((lane >= 16) | (rowc >= sh))
            s = s + jnp.where(ok, pltpu.roll(s, sh, 0), 0.0)
            sh *= 2
        sc = s + fc_sc[...]
        lane1 = lax.broadcasted_iota(jnp.int32, (1, LANES), 1)
        fc_sc[...] = jnp.where((lane1 >= 16) & (lane1 < 32), sc[tm - 1:tm, :], 0.0)
        sct_ref[0] = sc.T
        smt_ref[0] = sm.T
        csa_ref[0] = _split3_packed(sc, lane, 0).astype(BF16)
        if aug:
            fa = _split3_packed(sc * LOG2E, lane, 16)
            state["fa"] = jnp.where(lane == SPLIT_ONE_LANE, 1.0, fa).astype(BF16)

    def post_xbc(c, res):
        xp_sc[8:8 + tm, width * c:width * (c + 1)] = res
        for j in range(2 * c, 2 * c + 2):
            sl = slice(LANES * j, LANES * (j + 1))
            acc = cb_ref[:, sl] + cw_ref[3:4, sl] * xp_sc[8:8 + tm, sl]
            acc = acc + cw_ref[2:3, sl] * xp_sc[7:7 + tm, sl]
            acc = acc + cw_ref[1:2, sl] * xp_sc[6:6 + tm, sl]
            acc = acc + cw_ref[0:1, sl] * xp_sc[5:5 + tm, sl]
            xbc_ref[0, :, sl] = _silu(acc).astype(BF16)

    def post_gate(ref, c, res):
        ref[0, :, width * c:width * (c + 1)] = _silu(res).astype(BF16)

    def pair_halves(ref, c, res, half):
        for e in range(2):
            lo = 2 * LANES * (2 * c + e) + LANES * half
            ref[0, :, lo:lo + LANES] = res[:, LANES * e:LANES * (e + 1)].astype(BF16)

    def post_k(c, res):
        k_ref[0, :, width * c:width * (c + 1)] = res
        if aug:
            pair_halves(ka_ref, c, res, 0)

    def post_v(c, res):
        v_ref[0, :, width * c:width * (c + 1)] = res
        if aug:
            for r in range(tm // FOX_TK):
                vt_ref[0, r, width * c:width * (c + 1), :] = res[FOX_TK * r:FOX_TK * (r + 1), :].T.astype(BF16)

    def post_q(c, res):
        if aug:
            pair_halves(qa_ref, c, res * (ATTN_SCALE * LOG2E), 0)
        else:
            q16_ref[0, :, width * c:width * (c + 1)] = (res * ATTN_SCALE).astype(BF16)

    def post_ext(c, res):
        n_q = D_ATTN // width
        if c < n_q:
            pair_halves(qa_ref, c, res, 1)
        else:
            pair_halves(ka_ref, c - n_q, res, 1)

    def main_job(base, c, post):
        return (lambda: jnp.dot(xb, w_ref[:, base + width * c:base + width * (c + 1)],
                                preferred_element_type=F32), functools.partial(post, c))

    heavy = [main_job(C_XBC, c, post_xbc) for c in range(D_XBC // width)]
    heavy += [main_job(C_ZS, c, functools.partial(post_gate, gs_ref)) for c in range(D_SSM // width)]
    heavy += [main_job(C_ZA, c, functools.partial(post_gate, ga_ref)) for c in range(D_ATTN // width)]
    light = [main_job(C_K, c, post_k) for c in range(D_ATTN // width)]
    light += [main_job(C_V, c, post_v) for c in range(D_ATTN // width)]
    light += [main_job(C_Q, c, post_q) for c in range(D_ATTN // width)]
    if aug:
        light += [(lambda c=c: jnp.dot(state["fa"], pext_ref[:, width * c:width * (c + 1)],
                                       preferred_element_type=F32), functools.partial(post_ext, c))
                  for c in range(2 * D_ATTN // width)]
    jobs = [(lambda: jnp.dot(xb, w_ref[:, C_SM:N_PROJ], preferred_element_type=F32), post_small)]
    while heavy or light:
        jobs += light[:1] + heavy[:1]
        light, heavy = light[1:], heavy[1:]
    ahead = 2
    pending = [jobs[n][0]() for n in range(ahead)]
    for n, (_, post) in enumerate(jobs):
        res = pending.pop(0)
        if n + ahead < len(jobs):
            pending.append(jobs[n + ahead][0]())
        post(res)

    @pl.when(t == nt - 1)
    def _():
        cout_ref[0] = xp_sc[tm + 5:tm + 8, :]


def _const_spec(shape):
    return pl.BlockSpec(shape, lambda b, t: (0,) * len(shape), pipeline_mode=pl.Buffered(1))


def _proj_call(x, wp, cst, cw, cb, sbias, alog, pext, *, tm, chunk, aug):
    bsz, tlen, _ = x.shape
    assert tlen % tm == 0 and tm % chunk == 0 and chunk & (chunk - 1) == 0
    grid = (bsz, tlen // tm)

    def tok(width):
        return pl.BlockSpec((1, tm, width), lambda b, t: (b, t, 0))

    tr = pl.BlockSpec((1, LANES, tm), lambda b, t: (b, 0, t))
    per_b = pl.BlockSpec((1, CONV_W - 1, D_XBC), lambda b, t: (b, 0, 0))
    in_specs = [tok(D_MODEL), _const_spec((D_MODEL, N_PROJ)), per_b, _const_spec((CONV_W, D_XBC)),
                _const_spec((1, D_XBC)), _const_spec((1, LANES)), _const_spec((1, LANES))]
    args = [x, wp, cst, cw, cb, sbias, alog]

    def sds(width, dtype):
        return jax.ShapeDtypeStruct((bsz, tlen, width), dtype)

    out_shape = [sds(D_ATTN, F32), sds(D_ATTN, F32)]
    out_specs = [tok(D_ATTN), tok(D_ATTN)]
    if aug:
        in_specs.append(_const_spec((LANES, 2 * D_ATTN)))
        args.append(pext)
        assert tm % FOX_TK == 0
        out_shape += [sds(2 * D_ATTN, BF16), sds(2 * D_ATTN, BF16),
                      jax.ShapeDtypeStruct((bsz, tlen // FOX_TK, D_ATTN, FOX_TK), BF16)]
        out_specs += [tok(2 * D_ATTN), tok(2 * D_ATTN),
                      pl.BlockSpec((1, tm // FOX_TK, D_ATTN, FOX_TK), lambda b, t: (b, t, 0, 0))]
    else:
        out_shape += [sds(D_ATTN, BF16)]
        out_specs += [tok(D_ATTN)]
    out_shape += [sds(D_ATTN, BF16), sds(D_SSM, BF16), sds(D_XBC, BF16), sds(LANES, F32), sds(LANES, BF16),
                  jax.ShapeDtypeStruct((bsz, LANES, tlen), F32), jax.ShapeDtypeStruct((bsz, LANES, tlen), F32),
                  jax.ShapeDtypeStruct((bsz, CONV_W - 1, D_XBC), F32)]
    out_specs += [tok(D_ATTN), tok(D_SSM), tok(D_XBC), tok(LANES), tok(LANES), tr, tr, per_b]
    return pl.pallas_call(
        functools.partial(_proj_kernel, tm=tm, chunk=chunk, aug=aug),
        grid=grid, in_specs=in_specs, out_specs=out_specs, out_shape=out_shape,
        scratch_shapes=[pltpu.VMEM((tm + 8, D_XBC), F32), pltpu.VMEM((1, LANES), F32)],
        compiler_params=pltpu.CompilerParams(dimension_semantics=("arbitrary", "arbitrary"),
                                             vmem_limit_bytes=VMEM_LIMIT),
        name="proj_aug" if aug else "proj_dec",
    )(*args)


def _ssd_kernel(xbc_ref, gate_ref, csa_ref, sct_ref, smt_ref, h0_ref, prep_ref, pcd_ref, dskip_ref, nw_ref,
                y_ref, hout_ref, h_sc, y_sc, *, nsub):
    L = SSD_CHUNK
    t = pl.program_id(1)
    nt = pl.num_programs(1)

    @pl.when(t == 0)
    def _():
        h_sc[...] = h0_ref[0]

    lane = lax.broadcasted_iota(jnp.int32, (L, LANES), 1)
    low = lane < HEAD_DIM
    tri = lax.broadcasted_iota(jnp.int32, (L, L), 0) >= lax.broadcasted_iota(jnp.int32, (L, L), 1)
    zero16 = jnp.zeros((L, LANES), BF16)

    for j in range(nsub):
        rows = slice(j * L, (j + 1) * L)
        csa = csa_ref[0, rows, :]
        csrep = jnp.dot(csa, prep_ref[...], preferred_element_type=F32)
        cst = sct_ref[0, 0:N_HEADS, rows]
        dtt = smt_ref[0, 0:N_HEADS, rows]
        w2t = dtt * jnp.exp(cst[:, L - 1:L] - cst)
        cd = jnp.exp(jnp.dot(csa[L - 8:L, :], pcd_ref[...], preferred_element_type=F32)[7:8, :])
        for g in range(N_GROUPS):
            bg = xbc_ref[0, rows, D_SSM + D_STATE * g:D_SSM + D_STATE * (g + 1)]
            cg = xbc_ref[0, rows, D_SSM + D_BC + D_STATE * g:D_SSM + D_BC + D_STATE * (g + 1)]
            cb = lax.dot_general(cg, bg, (((1,), (1,)), ((), ())), preferred_element_type=F32)
            bgt = bg.astype(F32).T
            cgf = cg.astype(F32)

            def head_parts(h):
                rep = csrep[:, LANES * h:LANES * (h + 1)]
                dec = jnp.exp(jnp.where(tri, rep - cst[h:h + 1, :], -jnp.inf))
                m = (cb * dec * dtt[h:h + 1, :]).astype(BF16)
                ce = (cgf * jnp.exp(rep)).astype(BF16)
                btw = (bgt * w2t[h:h + 1, :]).astype(BF16)
                return m, ce, btw

            for cc in range(N_PAIRS // N_GROUPS):
                c = (N_PAIRS // N_GROUPS) * g + cc
                cols = slice(LANES * c, LANES * (c + 1))
                xs = xbc_ref[0, rows, cols]
                xa = jnp.where(low, xs, zero16)
                xb = jnp.where(low, zero16, xs)
                hp = h_sc[:, cols]
                hp16 = hp.astype(BF16)
                ha = jnp.where(low, hp16, zero16)
                hb = jnp.where(low, zero16, hp16)
                ma, cea, btwa = head_parts(2 * c)
                mb, ceb, btwb = head_parts(2 * c + 1)
                y = jnp.dot(jnp.concatenate([ma, cea], axis=1), jnp.concatenate([xa, ha], axis=0),
                            preferred_element_type=F32)
                y = y + jnp.dot(jnp.concatenate([mb, ceb], axis=1), jnp.concatenate([xb, hb], axis=0),
                                preferred_element_type=F32)
                st = jnp.dot(jnp.concatenate([btwa, btwb], axis=1), jnp.concatenate([xa, xb], axis=0),
                             preferred_element_type=F32)
                h_sc[:, cols] = hp * cd[:, cols] + st
                y_sc[rows, cols] = y + dskip_ref[:, cols] * xs.astype(F32)

    half = D_SSM // N_GROUPS
    for g in range(N_GROUPS):
        cols = slice(half * g, half * (g + 1))
        yg = y_sc[:, cols] * gate_ref[0, :, cols].astype(F32)
        ms = jnp.mean(yg * yg, axis=-1, keepdims=True)
        y_ref[0, :, cols] = (yg * lax.rsqrt(ms + RMS_EPS) * nw_ref[:, cols]).astype(BF16)

    @pl.when(t == nt - 1)
    def _():
        hout_ref[0] = h_sc[...]


def _ssd_call(xbc, gate, csa, sct, smt, h0t, prep, pcd, dskip, nw, *, nsub):
    bsz, tlen, _ = xbc.shape
    tmc = SSD_CHUNK * nsub
    assert tlen % tmc == 0
    grid = (bsz, tlen // tmc)

    def tok(width):
        return pl.BlockSpec((1, tmc, width), lambda b, t: (b, t, 0))

    tr = pl.BlockSpec((1, LANES, tmc), lambda b, t: (b, 0, t))
    st = pl.BlockSpec((1, D_STATE, D_SSM), lambda b, t: (b, 0, 0))
    return pl.pallas_call(
        functools.partial(_ssd_kernel, nsub=nsub),
        grid=grid,
        in_specs=[tok(D_XBC), tok(D_SSM), tok(LANES), tr, tr, st,
                  _const_spec((LANES, N_HEADS * LANES)), _const_spec((LANES, D_SSM)),
                  _const_spec((1, D_SSM)), _const_spec((1, D_SSM))],
        out_specs=[tok(D_SSM), st],
        out_shape=[jax.ShapeDtypeStruct((bsz, tlen, D_SSM), BF16),
                   jax.ShapeDtypeStruct((bsz, D_STATE, D_SSM), F32)],
        scratch_shapes=[pltpu.VMEM((D_STATE, D_SSM), F32), pltpu.VMEM((tmc, D_SSM), F32)],
        compiler_params=pltpu.CompilerParams(dimension_semantics=("arbitrary", "arbitrary"),
                                             vmem_limit_bytes=VMEM_LIMIT),
        name="ssd",
    )(xbc, gate, csa, sct, smt, h0t, prep, pcd, dskip, nw)


def _fox_kernel(qa_ref, ka_ref, vt_ref, g_ref, o_ref, qm_sc, m_sc, l_sc, acc_sc, st_sc, *, tq, kb, ahead):
    tk = FOX_TK
    i = pl.program_id(2)
    nslab = tq // tk
    assert nslab % kb == 0
    lane2 = lax.broadcasted_iota(jnp.int32, (1, 2 * LANES), 1)
    n_ext = 6
    in_a = (lane2 < HEAD_DIM) | ((lane2 >= LANES) & (lane2 < LANES + n_ext))
    in_b = ((lane2 >= HEAD_DIM) & (lane2 < LANES)) | ((lane2 >= LANES + n_ext) & (lane2 < LANES + 2 * n_ext))
    q2 = qa_ref[0]
    zq = jnp.zeros_like(q2)
    qm_sc[0] = jnp.where(in_a, q2, zq)
    qm_sc[1] = jnp.where(in_b, q2, zq)
    m_sc[...] = jnp.full_like(m_sc, MASK_VALUE)
    l_sc[...] = jnp.zeros_like(l_sc)
    acc_sc[...] = jnp.zeros_like(acc_sc)
    visible = lax.broadcasted_iota(jnp.int32, (tk, tk), 0) <= lax.broadcasted_iota(jnp.int32, (tk, tk), 1)
    nt_dims = (((1,), (1,)), ((), ()))
    ones_rows = jnp.ones((16, tk), BF16)

    def scores(unit):
        j, x, slab, _ = unit
        kblk = ka_ref[0, pl.ds(pl.multiple_of(j * tk, tk), tk), :]
        return lax.dot_general(kblk, qm_sc[x, tk * slab:tk * (slab + 1), :], nt_dims,
                               preferred_element_type=F32)

    def finish(unit, st):
        j, x, slab, masked = unit
        cols = slice(tk * slab, tk * (slab + 1))
        if masked:
            st = jnp.where(visible, st, MASK_VALUE)
        m_old = m_sc[x, :, cols]
        m_new = jnp.maximum(m_old, jnp.max(st, axis=0, keepdims=True))
        p = jnp.exp2(st - m_new).astype(BF16)
        alpha = jnp.exp2(m_old - m_new)
        vt1 = jnp.concatenate([vt_ref[0, j, HEAD_DIM * x:HEAD_DIM * (x + 1), :], ones_rows], axis=0)
        pv = jnp.dot(vt1, p, preferred_element_type=F32)
        l_sc[x, :, cols] = alpha * l_sc[x, :, cols] + pv[HEAD_DIM:HEAD_DIM + 1, :]
        acc_sc[x, :, cols] = alpha * acc_sc[x, :, cols] + pv[0:HEAD_DIM, :]
        m_sc[x, :, cols] = m_new

    def run(units, next_units):
        pending = []
        for n, unit in enumerate(units):
            st = st_sc[n] if n < ahead else pending.pop(0)
            if n + ahead < len(units):
                pending.append(scores(units[n + ahead]))
            elif next_units is not None:
                st_sc[n + ahead - len(units)] = scores(next_units[n + ahead - len(units)])
            finish(unit, st)

    def full_units(jo):
        return [(kb * jo + jb, x, slab, False) for jb in range(kb) for x in range(2) for slab in range(nslab)]

    tail_units = [(nslab * i + jj, x, slab, slab == jj)
                  for jj in range(nslab) for x in range(2) for slab in range(jj, nslab)]
    assert all(f[1:3] == t[1:3] for f, t in zip(full_units(0)[:ahead], tail_units[:ahead]))
    assert ahead <= 2 * nslab
    for a, unit in enumerate(full_units(0)[:ahead]):
        st_sc[a] = scores(unit)

    def body(jo, carry):
        run(full_units(jo), full_units(jo + 1))
        return carry

    lax.fori_loop(0, (nslab * i) // kb, body, 0)
    run(tail_units, None)
    ot = jnp.concatenate([acc_sc[0] / l_sc[0], acc_sc[1] / l_sc[1]], axis=0)
    o_ref[0] = (ot.T * g_ref[0].astype(F32)).astype(BF16)


def _fox_call(qa, ka, vt, ga, *, tq, kb, ahead):
    bsz, nkv, _, tk = vt.shape
    tlen = nkv * tk
    assert tlen % tq == 0 and tq % tk == 0 and tk == FOX_TK
    grid = (bsz, N_PAIRS, tlen // tq)
    return pl.pallas_call(
        functools.partial(_fox_kernel, tq=tq, kb=kb, ahead=ahead),
        grid=grid,
        in_specs=[pl.BlockSpec((1, tq, 2 * LANES), lambda b, c, i: (b, i, c)),
                  pl.BlockSpec((1, tlen, 2 * LANES), lambda b, c, i: (b, 0, c)),
                  pl.BlockSpec((1, nkv, LANES, tk), lambda b, c, i: (b, 0, c, 0)),
                  pl.BlockSpec((1, tq, LANES), lambda b, c, i: (b, i, c))],
        out_specs=pl.BlockSpec((1, tq, LANES), lambda b, c, i: (b, i, c)),
        out_shape=jax.ShapeDtypeStruct((bsz, tlen, D_ATTN), BF16),
        scratch_shapes=[pltpu.VMEM((2, tq, 2 * LANES), BF16),
                        pltpu.VMEM((2, 1, tq), F32), pltpu.VMEM((2, 1, tq), F32),
                        pltpu.VMEM((2, HEAD_DIM, tq), F32), pltpu.VMEM((ahead, tk, tk), F32)],
        compiler_params=pltpu.CompilerParams(dimension_semantics=("arbitrary", "arbitrary", "arbitrary"),
                                             vmem_limit_bytes=VMEM_LIMIT),
        name="fox_prompt",
    )(qa, ka, vt, ga)


def _fox_dec_kernel(q_ref, kc_ref, vc_ref, kn_ref, vn_ref, lft_ref, lfc_ref, g_ref, o_ref, *, past, tn, npad):
    width = past + npad
    fkt = lft_ref[0]
    lane_w = lax.broadcasted_iota(jnp.int32, fkt.shape, 1)
    sh = 1
    while sh < width:
        fkt = fkt + jnp.where(lane_w >= sh, pltpu.roll(fkt, sh, 1), 0.0)
        sh *= 2
    fcol = lfc_ref[0]
    row_w = lax.broadcasted_iota(jnp.int32, fcol.shape, 0)
    sh = 1
    while sh < width:
        fcol = fcol + jnp.where(row_w >= sh, pltpu.roll(fcol, sh, 0), 0.0)
        sh *= 2
    fq = fcol[past:past + tn, :]
    lane = lax.broadcasted_iota(jnp.int32, (tn, LANES), 1)
    low = lane < HEAD_DIM
    causal = lax.broadcasted_iota(jnp.int32, (tn, npad), 1) <= lax.broadcasted_iota(jnp.int32, (tn, npad), 0)
    nt_dims = (((1,), (1,)), ((), ()))
    for c in range(N_PAIRS):
        cols = slice(LANES * c, LANES * (c + 1))
        q2 = q_ref[0, :, cols]
        zq = jnp.zeros_like(q2)
        kc2 = kc_ref[0, :, cols].astype(BF16)
        vc2 = vc_ref[0, :, cols].astype(BF16)
        kn2 = kn_ref[0, :, cols].astype(BF16)
        vn2 = vn_ref[0, :, cols].astype(BF16)
        outs = []
        for x in range(2):
            h = 2 * c + x
            qx = jnp.where(low, q2, zq) if x == 0 else jnp.where(low, zq, q2)
            fqh = fq[:, h:h + 1]
            s_c = lax.dot_general(qx, kc2, nt_dims, preferred_element_type=F32) + fqh - fkt[h:h + 1, 0:past]
            s_n = lax.dot_general(qx, kn2, nt_dims, preferred_element_type=F32) + fqh - fkt[h:h + 1, past:width]
            s_n = jnp.where(causal, s_n, MASK_VALUE)
            m = jnp.maximum(jnp.max(s_c, axis=-1, keepdims=True), jnp.max(s_n, axis=-1, keepdims=True))
            p_c = jnp.exp(s_c - m)
            p_n = jnp.exp(s_n - m)
            den = jnp.sum(p_c, axis=-1, keepdims=True) + jnp.sum(p_n, axis=-1, keepdims=True)
            o = jnp.dot(p_c.astype(BF16), vc2, preferred_element_type=F32)
            o = o + jnp.dot(p_n.astype(BF16), vn2, preferred_element_type=F32)
            outs.append(o / den)
        o2 = jnp.where(low, outs[0], outs[1])
        o_ref[0, :, cols] = (o2 * g_ref[0, :, cols].astype(F32)).astype(BF16)


def _fox_dec_call(q16, kc, vc, kn, vn, lft, lfc, ga, *, tn):
    bsz, past, _ = kc.shape
    npad = kn.shape[1]
    width = past + npad

    def per_b(shape):
        return pl.BlockSpec((1,) + shape, lambda b: (b, 0, 0))

    return pl.pallas_call(
        functools.partial(_fox_dec_kernel, past=past, tn=tn, npad=npad),
        grid=(bsz,),
        in_specs=[per_b((tn, D_ATTN)), per_b((past, D_ATTN)), per_b((past, D_ATTN)),
                  per_b((npad, D_ATTN)), per_b((npad, D_ATTN)),
                  per_b((N_HEADS, width)), per_b((width, LANES)), per_b((tn, D_ATTN))],
        out_specs=per_b((tn, D_ATTN)),
        out_shape=jax.ShapeDtypeStruct((bsz, tn, D_ATTN), BF16),
        compiler_params=pltpu.CompilerParams(dimension_semantics=("arbitrary",),
                                             vmem_limit_bytes=VMEM_LIMIT),
        name="fox_dec",
    )(q16, kc, vc, kn, vn, lft, lfc, ga)


def _out_kernel(ys_ref, ya_ref, x_ref, w_ref, g_ref, b_ref, o_ref, *, alpha):
    mixed = jnp.dot(ys_ref[...], w_ref[0:D_SSM, :], preferred_element_type=F32)
    mixed = mixed + jnp.dot(ya_ref[...], w_ref[D_SSM:D_SSM + D_ATTN, :], preferred_element_type=F32)
    h = alpha * x_ref[...] + mixed
    mu = jnp.mean(h, axis=-1, keepdims=True)
    hc = h - mu
    var = jnp.mean(hc * hc, axis=-1, keepdims=True)
    o_ref[...] = hc * lax.rsqrt(var + LN_EPS) * g_ref[...] + b_ref[...]


def _out_call(ys, ya, x, w16, g, b, *, alpha, tm):
    m = x.shape[0]
    assert m % tm == 0
    row = pl.BlockSpec((tm, D_MODEL), lambda i: (i, 0))

    def const(shape):
        return pl.BlockSpec(shape, lambda i: (0, 0), pipeline_mode=pl.Buffered(1))

    return pl.pallas_call(
        functools.partial(_out_kernel, alpha=alpha),
        grid=(m // tm,),
        in_specs=[row, row, row, const((D_SSM + D_ATTN, D_MODEL)), const((1, D_MODEL)), const((1, D_MODEL))],
        out_specs=row,
        out_shape=jax.ShapeDtypeStruct((m, D_MODEL), F32),
        compiler_params=pltpu.CompilerParams(dimension_semantics=("arbitrary",),
                                             vmem_limit_bytes=VMEM_LIMIT),
        name="out_proj",
    )(ys, ya, x, w16, g, b)


def _placement_constants():
    prep = np.zeros((LANES, N_HEADS * LANES), np.float32)
    pcd = np.zeros((LANES, D_SSM), np.float32)
    pext = np.zeros((LANES, 2 * D_ATTN), np.float32)
    for h in range(N_HEADS):
        for part in range(3):
            prep[16 * part + h, LANES * h:LANES * (h + 1)] = 1.0
            pcd[16 * part + h, HEAD_DIM * h:HEAD_DIM * (h + 1)] = 1.0
        base = LANES * (h // 2) + 6 * (h % 2)
        for part in range(3):
            pext[16 * part + h, base + part] = 1.0
            pext[SPLIT_ONE_LANE, base + 3 + part] = 1.0
            pext[SPLIT_ONE_LANE, D_ATTN + base + part] = 1.0
            pext[16 * part + h, D_ATTN + base + 3 + part] = -1.0
    return jnp.asarray(prep, BF16), jnp.asarray(pcd, BF16), jnp.asarray(pext, BF16)


def _permute_w_in(w):
    z_ssm, xbc = w[:, 0:1024], w[:, 1024:2560]
    dt, q, k, v = w[:, 2560:2576], w[:, 2576:3600], w[:, 3600:4624], w[:, 4624:5648]
    z_attn, f = w[:, 5648:6672], w[:, 6672:6688]
    pad = jnp.zeros((w.shape[0], LANES - 2 * N_HEADS), w.dtype)
    return jnp.concatenate([z_ssm, xbc, q, k, v, z_attn, dt, f, pad], axis=1).astype(BF16)


def _pad_rows(a, n, mode):
    return jnp.pad(a, ((0, 0), (0, n - a.shape[1]), (0, 0)), mode=mode)


def _state_to_kernel(h):
    return jnp.transpose(h, (0, 3, 1, 2)).reshape(h.shape[0], D_STATE, D_SSM)


def _state_from_kernel(ht):
    return jnp.transpose(ht.reshape(ht.shape[0], D_STATE, N_HEADS, HEAD_DIM), (0, 2, 3, 1))


def _layer(x, conv_state, ssm_state, past_k, past_v, past_logf, p, *, depth, prompt):
    bsz, tlen, _ = x.shape
    prep, pcd, pext = _placement_constants()
    alpha = (2 * depth) ** 0.25
    if prompt:
        tm = min(512, tlen)
        (k, v, qa, ka, vt, ga, gs, xbc, sm, csa, sct, smt, cout) = _proj_call(
            x, p["wp"], conv_state, p["cw"], p["cb"], p["sbias"], p["alog"], pext,
            tm=tm, chunk=SSD_CHUNK, aug=True)
        nsub = 2 if tlen % (2 * SSD_CHUNK) == 0 else 1
        y_ssm, ht = _ssd_call(xbc, gs, csa, sct, smt, _state_to_kernel(ssm_state), prep, pcd,
                              p["dskip"], p["nw"], nsub=nsub)
        y_attn = _fox_call(qa, ka, vt, ga, tq=min(1024, tlen), kb=4, ahead=4)
    else:
        assert tlen <= SSD_CHUNK
        (k, v, q16, ga, gs, xbc, sm, csa, sct, smt, cout) = _proj_call(
            x, p["wp"], conv_state, p["cw"], p["cb"], p["sbias"], p["alog"], None,
            tm=tlen, chunk=tlen, aug=False)
        y_ssm, ht = _ssd_call(
            _pad_rows(xbc, SSD_CHUNK, "constant"), _pad_rows(gs, SSD_CHUNK, "constant"),
            _pad_rows(csa, SSD_CHUNK, "edge"),
            jnp.pad(sct, ((0, 0), (0, 0), (0, SSD_CHUNK - tlen)), mode="edge"),
            jnp.pad(smt, ((0, 0), (0, 0), (0, SSD_CHUNK - tlen))),
            _state_to_kernel(ssm_state), prep, pcd, p["dskip"], p["nw"], nsub=1)
        y_ssm = y_ssm[:, :tlen]
        past = past_k.shape[1]
        lf_all = jnp.concatenate([past_logf, sm[:, :, 16:32],
                                  jnp.zeros((bsz, LANES - tlen, N_HEADS), F32)], axis=1)
        lft = jnp.transpose(lf_all, (0, 2, 1))
        lfc = jnp.pad(lf_all, ((0, 0), (0, 0), (0, LANES - N_HEADS)))
        y_attn = _fox_dec_call(q16, past_k.reshape(bsz, past, D_ATTN), past_v.reshape(bsz, past, D_ATTN),
                               _pad_rows(k, LANES, "constant"), _pad_rows(v, LANES, "constant"),
                               lft, lfc, ga, tn=tlen)
    m = bsz * tlen
    y = _out_call(y_ssm.reshape(m, D_SSM), y_attn.reshape(m, D_ATTN), x.reshape(m, D_MODEL),
                  p["wo"], p["ln_g"], p["ln_b"], alpha=alpha, tm=min(512, m))
    return (y.reshape(bsz, tlen, D_MODEL), cout, _state_from_kernel(ht),
            k.reshape(bsz, tlen, N_HEADS, HEAD_DIM), v.reshape(bsz, tlen, N_HEADS, HEAD_DIM),
            jnp.swapaxes(smt[:, 16:32, :], 1, 2))


def kernel(x_prompt, x_sample, cache_k, cache_v, cache_logf, state_conv, state_ssm, w_in, conv_w, conv_b,
           dt_bias, a_log, d_skip, ssm_norm_w, f_bias, w_out, ln_g, ln_b):
    depth = w_in.shape[0]
    yp, ys = x_prompt, x_sample
    bp = x_prompt.shape[0]
    outs_p, outs_s = [], []
    for layer in range(depth):
        zpad = jnp.zeros((1, LANES - 2 * N_HEADS), F32)
        p = dict(
            wp=_permute_w_in(w_in[layer]),
            cw=conv_w[layer], cb=conv_b[layer][None, :],
            sbias=jnp.concatenate([dt_bias[layer][None, :], f_bias[layer][None, :], zpad], axis=1),
            alog=jnp.concatenate([a_log[layer][None, :], jnp.zeros((1, LANES - N_HEADS), F32)], axis=1),
            dskip=jnp.repeat(d_skip[layer], HEAD_DIM)[None, :], nw=ssm_norm_w[layer][None, :],
            wo=w_out[layer].astype(BF16), ln_g=ln_g[layer][None, :], ln_b=ln_b[layer][None, :])
        conv0 = jnp.zeros((bp, CONV_W - 1, D_XBC), F32)
        ssm0 = jnp.zeros((bp, N_HEADS, HEAD_DIM, D_STATE), state_ssm.dtype)
        yp, cp, sp, kp, vp, fp = _layer(yp, conv0, ssm0, None, None, None, p, depth=depth, prompt=True)
        ys, cs, ss, kn, vn, fn = _layer(ys, state_conv[layer], state_ssm[layer], cache_k[layer], cache_v[layer],
                                        cache_logf[layer], p, depth=depth, prompt=False)
        outs_p.append((kp, vp, fp, cp, sp))
        outs_s.append((kn, vn, fn, cs, ss))

    def stack(items, idx):
        return jnp.stack([it[idx] for it in items])

    return (yp, ys, stack(outs_p, 0), stack(outs_p, 1), stack(outs_p, 2), stack(outs_p, 3), stack(outs_p, 4),
            stack(outs_s, 0), stack(outs_s, 1), stack(outs_s, 2), stack(outs_s, 3), stack(outs_s, 4))
```

```python
import functools

import numpy as np
import jax
import jax.numpy as jnp
from jax import lax
from jax.experimental import pallas as pl
from jax.experimental.pallas import tpu as pltpu

F32 = jnp.float32
BF16 = jnp.bfloat16

D_MODEL = 1024
D_SSM = 1024
D_ATTN = 1024
HEAD_DIM = 64
N_HEADS = 16
N_PAIRS = N_HEADS // 2
N_GROUPS = 2
D_STATE = 128
D_BC = N_GROUPS * D_STATE
CONV_W = 4
D_XBC = D_SSM + 2 * D_BC
ATTN_SCALE = HEAD_DIM ** -0.5
LN_EPS = 1e-5
RMS_EPS = 1e-5
LANES = 128
SSD_CHUNK = 128
MASK_VALUE = -1e30
LOG2E = 1.4426950408889634
FOX_TK = 256
VMEM_LIMIT = 56 * 1024 * 1024

C_ZS, C_XBC, C_Q, C_K, C_V, C_ZA, C_SM = 0, 1024, 2560, 3584, 4608, 5632, 6656
N_PROJ = C_SM + LANES
SPLIT_ONE_LANE = 3 * N_HEADS
N_EXT = 6


def _silu(z):
    return z * jax.nn.sigmoid(z)


def _split3_packed(val, lane, base):
    hi = val.astype(BF16).astype(F32)
    r1 = val - hi
    mid = r1.astype(BF16).astype(F32)
    lo = r1 - mid

    def to(v, dst):
        shift = (dst - base) % LANES
        return v if shift == 0 else pltpu.roll(v, shift, 1)

    return jnp.where(lane < 16, to(hi, 0),
                     jnp.where(lane < 32, to(mid, 16),
                               jnp.where(lane < 48, to(lo, 32), 0.0)))


def _proj_kernel(*refs, tm, chunk, aug):
    if aug:
        (x_ref, w_ref, cst_ref, cw_ref, cb_ref, sbias_ref, alog_ref, pext_ref,
         k_ref, v_ref, qt_ref, ka_ref, vt_ref, ga_ref, gs_ref, xbc_ref, sm_ref, csa_ref,
         sct_ref, smt_ref, cout_ref, xp_sc, fc_sc) = refs
    else:
        (x_ref, w_ref, cst_ref, cw_ref, cb_ref, sbias_ref, alog_ref,
         k_ref, v_ref, q16_ref, ga_ref, gs_ref, xbc_ref, sm_ref, csa_ref,
         sct_ref, smt_ref, cout_ref, xp_sc, fc_sc) = refs
    t = pl.program_id(1)
    nt = pl.num_programs(1)
    width = 2 * LANES

    @pl.when(t == 0)
    def _():
        fc_sc[...] = jnp.zeros_like(fc_sc)
        xp_sc[5:8, :] = cst_ref[0]

    @pl.when(t > 0)
    def _():
        xp_sc[5:8, :] = xp_sc[tm + 5:tm + 8, :]

    xb = x_ref[0].astype(BF16)
    lane = lax.broadcasted_iota(jnp.int32, (tm, LANES), 1)
    state = {}

    def post_small(res):
        row = lax.broadcasted_iota(jnp.int32, (tm, LANES), 0)
        u = res + sbias_ref[...]
        dt = jax.nn.softplus(u)
        logf = jax.nn.log_sigmoid(u)
        sm = jnp.where(lane < 16, dt, jnp.where(lane < 32, logf, 0.0))
        sm_ref[0] = sm
        aneg = -jnp.exp(alog_ref[...])
        s = jnp.where(lane < 16, dt * aneg, jnp.where(lane < 32, logf, 0.0))
        rowc = row & (chunk - 1)
        sh = 1
        while sh < tm:
            ok = (row >= sh) & ((lane >= 16) | (rowc >= sh))
            s = s + jnp.where(ok, pltpu.roll(s, sh, 0), 0.0)
            sh *= 2
        sc = s + fc_sc[...]
        lane1 = lax.broadcasted_iota(jnp.int32, (1, LANES), 1)
        fc_sc[...] = jnp.where((lane1 >= 16) & (lane1 < 32), sc[tm - 1:tm, :], 0.0)
        sct_ref[0] = sc.T
        smt_ref[0] = sm.T
        csa_ref[0] = _split3_packed(sc, lane, 0).astype(BF16)
        if aug:
            fa = _split3_packed(sc * LOG2E, lane, 16)
            state["fa"] = jnp.where(lane == SPLIT_ONE_LANE, 1.0, fa).astype(BF16)

    def post_xbc(c, res):
        xp_sc[8:8 + tm, width * c:width * (c + 1)] = res
        for j in range(2 * c, 2 * c + 2):
            sl = slice(LANES * j, LANES * (j + 1))
            acc = cb_ref[:, sl] + cw_ref[3:4, sl] * xp_sc[8:8 + tm, sl]
            acc = acc + cw_ref[2:3, sl] * xp_sc[7:7 + tm, sl]
            acc = acc + cw_ref[1:2, sl] * xp_sc[6:6 + tm, sl]
            acc = acc + cw_ref[0:1, sl] * xp_sc[5:5 + tm, sl]
            xbc_ref[0, :, sl] = _silu(acc).astype(BF16)

    def post_gate(ref, c, res):
        ref[0, :, width * c:width * (c + 1)] = _silu(res).astype(BF16)

    def pair_halves(ref, c, res, half):
        for e in range(2):
            lo = 2 * LANES * (2 * c + e) + LANES * half
            ref[0, :, lo:lo + LANES] = res[:, LANES * e:LANES * (e + 1)].astype(BF16)

    def post_k(c, res):
        k_ref[0, :, width * c:width * (c + 1)] = res
        if aug:
            pair_halves(ka_ref, c, res, 0)

    def post_v(c, res):
        v_ref[0, :, width * c:width * (c + 1)] = res
        if aug:
            for r in range(tm // FOX_TK):
                vt_ref[0, r, width * c:width * (c + 1), :] = res[FOX_TK * r:FOX_TK * (r + 1), :].T.astype(BF16)

    def head_rows_t(c, res, half, n_rows):
        rowt = lax.broadcasted_iota(jnp.int32, (LANES, tm), 0)
        for e in range(2):
            blk = res[:, LANES * e:LANES * (e + 1)].T
            for x in range(2):
                own = (rowt >= n_rows * x) & (rowt < n_rows * (x + 1))
                qt_ref[0, 2 * (2 * c + e) + x, LANES * half:LANES * (half + 1), :] = (
                    jnp.where(own, blk, 0.0).astype(BF16))

    def post_q(c, res):
        if aug:
            head_rows_t(c, res * (ATTN_SCALE * LOG2E), 0, HEAD_DIM)
        else:
            q16_ref[0, :, width * c:width * (c + 1)] = (res * ATTN_SCALE).astype(BF16)

    def post_ext(c, res):
        n_q = D_ATTN // width
        if c < n_q:
            head_rows_t(c, res, 1, N_EXT)
        else:
            pair_halves(ka_ref, c - n_q, res, 1)

    def main_job(base, c, post):
        return (lambda: jnp.dot(xb, w_ref[:, base + width * c:base + width * (c + 1)],
                                preferred_element_type=F32), functools.partial(post, c))

    heavy = [main_job(C_XBC, c, post_xbc) for c in range(D_XBC // width)]
    heavy += [main_job(C_ZS, c, functools.partial(post_gate, gs_ref)) for c in range(D_SSM // width)]
    heavy += [main_job(C_ZA, c, functools.partial(post_gate, ga_ref)) for c in range(D_ATTN // width)]
    light = [main_job(C_K, c, post_k) for c in range(D_ATTN // width)]
    light += [main_job(C_V, c, post_v) for c in range(D_ATTN // width)]
    light += [main_job(C_Q, c, post_q) for c in range(D_ATTN // width)]
    if aug:
        light += [(lambda c=c: jnp.dot(state["fa"], pext_ref[:, width * c:width * (c + 1)],
                                       preferred_element_type=F32), functools.partial(post_ext, c))
                  for c in range(2 * D_ATTN // width)]
    jobs = [(lambda: jnp.dot(xb, w_ref[:, C_SM:N_PROJ], preferred_element_type=F32), post_small)]
    while heavy or light:
        jobs += light[:1] + heavy[:1]
        light, heavy = light[1:], heavy[1:]
    ahead = 2
    pending = [jobs[n][0]() for n in range(ahead)]
    for n, (_, post) in enumerate(jobs):
        res = pending.pop(0)
        if n + ahead < len(jobs):
            pending.append(jobs[n + ahead][0]())
        post(res)

    @pl.when(t == nt - 1)
    def _():
        cout_ref[0] = xp_sc[tm + 5:tm + 8, :]


def _const_spec(shape):
    return pl.BlockSpec(shape, lambda b, t: (0,) * len(shape), pipeline_mode=pl.Buffered(1))


def _proj_call(x, wp, cst, cw, cb, sbias, alog, pext, *, tm, chunk, aug):
    bsz, tlen, _ = x.shape
    assert tlen % tm == 0 and tm % chunk == 0 and chunk & (chunk - 1) == 0
    grid = (bsz, tlen // tm)

    def tok(width):
        return pl.BlockSpec((1, tm, width), lambda b, t: (b, t, 0))

    tr = pl.BlockSpec((1, LANES, tm), lambda b, t: (b, 0, t))
    per_b = pl.BlockSpec((1, CONV_W - 1, D_XBC), lambda b, t: (b, 0, 0))
    in_specs = [tok(D_MODEL), _const_spec((D_MODEL, N_PROJ)), per_b, _const_spec((CONV_W, D_XBC)),
                _const_spec((1, D_XBC)), _const_spec((1, LANES)), _const_spec((1, LANES))]
    args = [x, wp, cst, cw, cb, sbias, alog]

    def sds(width, dtype):
        return jax.ShapeDtypeStruct((bsz, tlen, width), dtype)

    out_shape = [sds(D_ATTN, F32), sds(D_ATTN, F32)]
    out_specs = [tok(D_ATTN), tok(D_ATTN)]
    if aug:
        in_specs.append(_const_spec((LANES, 2 * D_ATTN)))
        args.append(pext)
        assert tm % FOX_TK == 0
        out_shape += [jax.ShapeDtypeStruct((bsz, N_HEADS, 2 * LANES, tlen), BF16), sds(2 * D_ATTN, BF16),
                      jax.ShapeDtypeStruct((bsz, tlen // FOX_TK, D_ATTN, FOX_TK), BF16)]
        out_specs += [pl.BlockSpec((1, N_HEADS, 2 * LANES, tm), lambda b, t: (b, 0, 0, t)), tok(2 * D_ATTN),
                      pl.BlockSpec((1, tm // FOX_TK, D_ATTN, FOX_TK), lambda b, t: (b, t, 0, 0))]
    else:
        out_shape += [sds(D_ATTN, BF16)]
        out_specs += [tok(D_ATTN)]
    out_shape += [sds(D_ATTN, BF16), sds(D_SSM, BF16), sds(D_XBC, BF16), sds(LANES, F32), sds(LANES, BF16),
                  jax.ShapeDtypeStruct((bsz, LANES, tlen), F32), jax.ShapeDtypeStruct((bsz, LANES, tlen), F32),
                  jax.ShapeDtypeStruct((bsz, CONV_W - 1, D_XBC), F32)]
    out_specs += [tok(D_ATTN), tok(D_SSM), tok(D_XBC), tok(LANES), tok(LANES), tr, tr, per_b]
    return pl.pallas_call(
        functools.partial(_proj_kernel, tm=tm, chunk=chunk, aug=aug),
        grid=grid, in_specs=in_specs, out_specs=out_specs, out_shape=out_shape,
        scratch_shapes=[pltpu.VMEM((tm + 8, D_XBC), F32), pltpu.VMEM((1, LANES), F32)],
        compiler_params=pltpu.CompilerParams(dimension_semantics=("arbitrary", "arbitrary"),
                                             vmem_limit_bytes=VMEM_LIMIT),
        name="proj_aug" if aug else "proj_dec",
    )(*args)


def _ssd_kernel(xbc_ref, gate_ref, csa_ref, sct_ref, smt_ref, h0_ref, prep_ref, pcd_ref, dskip_ref, nw_ref,
                y_ref, hout_ref, h_sc, y_sc, *, nsub):
    L = SSD_CHUNK
    t = pl.program_id(1)
    nt = pl.num_programs(1)

    @pl.when(t == 0)
    def _():
        h_sc[...] = h0_ref[0]

    lane = lax.broadcasted_iota(jnp.int32, (L, LANES), 1)
    low = lane < HEAD_DIM
    tri = lax.broadcasted_iota(jnp.int32, (L, L), 0) >= lax.broadcasted_iota(jnp.int32, (L, L), 1)
    zero16 = jnp.zeros((L, LANES), BF16)

    for j in range(nsub):
        rows = slice(j * L, (j + 1) * L)
        csa = csa_ref[0, rows, :]
        csrep = jnp.dot(csa, prep_ref[...], preferred_element_type=F32)
        cst = sct_ref[0, 0:N_HEADS, rows]
        dtt = smt_ref[0, 0:N_HEADS, rows]
        w2t = dtt * jnp.exp(cst[:, L - 1:L] - cst)
        cd = jnp.exp(jnp.dot(csa[L - 8:L, :], pcd_ref[...], preferred_element_type=F32)[7:8, :])
        for g in range(N_GROUPS):
            bg = xbc_ref[0, rows, D_SSM + D_STATE * g:D_SSM + D_STATE * (g + 1)]
            cg = xbc_ref[0, rows, D_SSM + D_BC + D_STATE * g:D_SSM + D_BC + D_STATE * (g + 1)]
            cb = lax.dot_general(cg, bg, (((1,), (1,)), ((), ())), preferred_element_type=F32)
            bgt = bg.astype(F32).T
            cgf = cg.astype(F32)

            def head_parts(h):
                rep = csrep[:, LANES * h:LANES * (h + 1)]
                dec = jnp.exp(jnp.where(tri, rep - cst[h:h + 1, :], -jnp.inf))
                m = (cb * dec * dtt[h:h + 1, :]).astype(BF16)
                ce = (cgf * jnp.exp(rep)).astype(BF16)
                btw = (bgt * w2t[h:h + 1, :]).astype(BF16)
                return m, ce, btw

            for cc in range(N_PAIRS // N_GROUPS):
                c = (N_PAIRS // N_GROUPS) * g + cc
                cols = slice(LANES * c, LANES * (c + 1))
                xs = xbc_ref[0, rows, cols]
                xa = jnp.where(low, xs, zero16)
                xb = jnp.where(low, zero16, xs)
                hp = h_sc[:, cols]
                hp16 = hp.astype(BF16)
                ha = jnp.where(low, hp16, zero16)
                hb = jnp.where(low, zero16, hp16)
                ma, cea, btwa = head_parts(2 * c)
                mb, ceb, btwb = head_parts(2 * c + 1)
                y = jnp.dot(jnp.concatenate([ma, cea], axis=1), jnp.concatenate([xa, ha], axis=0),
                            preferred_element_type=F32)
                y = y + jnp.dot(jnp.concatenate([mb, ceb], axis=1), jnp.concatenate([xb, hb], axis=0),
                                preferred_element_type=F32)
                st = jnp.dot(jnp.concatenate([btwa, btwb], axis=1), jnp.concatenate([xa, xb], axis=0),
                             preferred_element_type=F32)
                h_sc[:, cols] = hp * cd[:, cols] + st
                y_sc[rows, cols] = y + dskip_ref[:, cols] * xs.astype(F32)

    half = D_SSM // N_GROUPS
    for g in range(N_GROUPS):
        cols = slice(half * g, half * (g + 1))
        yg = y_sc[:, cols] * gate_ref[0, :, cols].astype(F32)
        ms = jnp.mean(yg * yg, axis=-1, keepdims=True)
        y_ref[0, :, cols] = (yg * lax.rsqrt(ms + RMS_EPS) * nw_ref[:, cols]).astype(BF16)

    @pl.when(t == nt - 1)
    def _():
        hout_ref[0] = h_sc[...]


def _ssd_call(xbc, gate, csa, sct, smt, h0t, prep, pcd, dskip, nw, *, nsub):
    bsz, tlen, _ = xbc.shape
    tmc = SSD_CHUNK * nsub
    assert tlen % tmc == 0
    grid = (bsz, tlen // tmc)

    def tok(width):
        return pl.BlockSpec((1, tmc, width), lambda b, t: (b, t, 0))

    tr = pl.BlockSpec((1, LANES, tmc), lambda b, t: (b, 0, t))
    st = pl.BlockSpec((1, D_STATE, D_SSM), lambda b, t: (b, 0, 0))
    return pl.pallas_call(
        functools.partial(_ssd_kernel, nsub=nsub),
        grid=grid,
        in_specs=[tok(D_XBC), tok(D_SSM), tok(LANES), tr, tr, st,
                  _const_spec((LANES, N_HEADS * LANES)), _const_spec((LANES, D_SSM)),
                  _const_spec((1, D_SSM)), _const_spec((1, D_SSM))],
        out_specs=[tok(D_SSM), st],
        out_shape=[jax.ShapeDtypeStruct((bsz, tlen, D_SSM), BF16),
                   jax.ShapeDtypeStruct((bsz, D_STATE, D_SSM), F32)],
        scratch_shapes=[pltpu.VMEM((D_STATE, D_SSM), F32), pltpu.VMEM((tmc, D_SSM), F32)],
        compiler_params=pltpu.CompilerParams(dimension_semantics=("arbitrary", "arbitrary"),
                                             vmem_limit_bytes=VMEM_LIMIT),
        name="ssd",
    )(xbc, gate, csa, sct, smt, h0t, prep, pcd, dskip, nw)


def _fox_kernel(qt_ref, ka_ref, vt_ref, g_ref, o_ref, m_sc, l_sc, acc_sc, st_sc, *, tq, kb, ahead):
    tk = FOX_TK
    i = pl.program_id(2)
    nslab = tq // tk
    assert nslab % kb == 0
    m_sc[...] = jnp.full_like(m_sc, MASK_VALUE)
    l_sc[...] = jnp.zeros_like(l_sc)
    acc_sc[...] = jnp.zeros_like(acc_sc)
    visible = lax.broadcasted_iota(jnp.int32, (tk, tk), 0) <= lax.broadcasted_iota(jnp.int32, (tk, tk), 1)
    ones_rows = jnp.ones((16, tk), BF16)

    def scores(unit):
        j, x, slab, _ = unit
        kblk = ka_ref[0, pl.ds(pl.multiple_of(j * tk, tk), tk), :]
        return jnp.dot(kblk, qt_ref[0, x, :, tk * slab:tk * (slab + 1)], preferred_element_type=F32)

    def finish(unit, st):
        j, x, slab, masked = unit
        cols = slice(tk * slab, tk * (slab + 1))
        if masked:
            st = jnp.where(visible, st, MASK_VALUE)
        m_old = m_sc[x, :, cols]
        m_new = jnp.maximum(m_old, jnp.max(st, axis=0, keepdims=True))
        p = jnp.exp2(st - m_new).astype(BF16)
        alpha = jnp.exp2(m_old - m_new)
        vt1 = jnp.concatenate([vt_ref[0, j, HEAD_DIM * x:HEAD_DIM * (x + 1), :], ones_rows], axis=0)
        pv = jnp.dot(vt1, p, preferred_element_type=F32)
        l_sc[x, :, cols] = alpha * l_sc[x, :, cols] + pv[HEAD_DIM:HEAD_DIM + 1, :]
        acc_sc[x, :, cols] = alpha * acc_sc[x, :, cols] + pv[0:HEAD_DIM, :]
        m_sc[x, :, cols] = m_new

    def run(units, next_units):
        pending = []
        for n, unit in enumerate(units):
            st = st_sc[n] if n < ahead else pending.pop(0)
            if n + ahead < len(units):
                pending.append(scores(units[n + ahead]))
            elif next_units is not None:
                st_sc[n + ahead - len(units)] = scores(next_units[n + ahead - len(units)])
            finish(unit, st)

    def full_units(jo):
        return [(kb * jo + jb, x, slab, False) for jb in range(kb) for x in range(2) for slab in range(nslab)]

    tail_units = [(nslab * i + jj, x, slab, slab == jj)
                  for jj in range(nslab) for x in range(2) for slab in range(jj, nslab)]
    assert all(f[1:3] == t[1:3] for f, t in zip(full_units(0)[:ahead], tail_units[:ahead]))
    assert ahead <= 2 * nslab
    for a, unit in enumerate(full_units(0)[:ahead]):
        st_sc[a] = scores(unit)

    def body(jo, carry):
        run(full_units(jo), full_units(jo + 1))
        return carry

    lax.fori_loop(0, (nslab * i) // kb, body, 0)
    run(tail_units, None)
    ot = jnp.concatenate([acc_sc[0] / l_sc[0], acc_sc[1] / l_sc[1]], axis=0)
    o_ref[0] = (ot.T * g_ref[0].astype(F32)).astype(BF16)


def _fox_call(qt, ka, vt, ga, *, tq, kb, ahead):
    bsz, nkv, _, tk = vt.shape
    tlen = nkv * tk
    assert tlen % tq == 0 and tq % tk == 0 and tk == FOX_TK
    grid = (bsz, N_PAIRS, tlen // tq)
    return pl.pallas_call(
        functools.partial(_fox_kernel, tq=tq, kb=kb, ahead=ahead),
        grid=grid,
        in_specs=[pl.BlockSpec((1, 2, 2 * LANES, tq), lambda b, c, i: (b, c, 0, i)),
                  pl.BlockSpec((1, tlen, 2 * LANES), lambda b, c, i: (b, 0, c)),
                  pl.BlockSpec((1, nkv, LANES, tk), lambda b, c, i: (b, 0, c, 0)),
                  pl.BlockSpec((1, tq, LANES), lambda b, c, i: (b, i, c))],
        out_specs=pl.BlockSpec((1, tq, LANES), lambda b, c, i: (b, i, c)),
        out_shape=jax.ShapeDtypeStruct((bsz, tlen, D_ATTN), BF16),
        scratch_shapes=[pltpu.VMEM((2, 1, tq), F32), pltpu.VMEM((2, 1, tq), F32),
                        pltpu.VMEM((2, HEAD_DIM, tq), F32), pltpu.VMEM((ahead, tk, tk), F32)],
        compiler_params=pltpu.CompilerParams(dimension_semantics=("arbitrary", "arbitrary", "arbitrary"),
                                             vmem_limit_bytes=VMEM_LIMIT),
        name="fox_prompt",
    )(qt, ka, vt, ga)


def _fox_dec_kernel(q_ref, kc_ref, vc_ref, kn_ref, vn_ref, lft_ref, lfc_ref, g_ref, o_ref, *, past, tn, npad):
    width = past + npad
    fkt = lft_ref[0]
    lane_w = lax.broadcasted_iota(jnp.int32, fkt.shape, 1)
    sh = 1
    while sh < width:
        fkt = fkt + jnp.where(lane_w >= sh, pltpu.roll(fkt, sh, 1), 0.0)
        sh *= 2
    fcol = lfc_ref[0]
    row_w = lax.broadcasted_iota(jnp.int32, fcol.shape, 0)
    sh = 1
    while sh < width:
        fcol = fcol + jnp.where(row_w >= sh, pltpu.roll(fcol, sh, 0), 0.0)
        sh *= 2
    fq = fcol[past:past + tn, :]
    lane = lax.broadcasted_iota(jnp.int32, (tn, LANES), 1)
    low = lane < HEAD_DIM
    causal = lax.broadcasted_iota(jnp.int32, (tn, npad), 1) <= lax.broadcasted_iota(jnp.int32, (tn, npad), 0)
    nt_dims = (((1,), (1,)), ((), ()))
    for c in range(N_PAIRS):
        cols = slice(LANES * c, LANES * (c + 1))
        q2 = q_ref[0, :, cols]
        zq = jnp.zeros_like(q2)
        kc2 = kc_ref[0, :, cols].astype(BF16)
        vc2 = vc_ref[0, :, cols].astype(BF16)
        kn2 = kn_ref[0, :, cols].astype(BF16)
        vn2 = vn_ref[0, :, cols].astype(BF16)
        outs = []
        for x in range(2):
            h = 2 * c + x
            qx = jnp.where(low, q2, zq) if x == 0 else jnp.where(low, zq, q2)
            fqh = fq[:, h:h + 1]
            s_c = lax.dot_general(qx, kc2, nt_dims, preferred_element_type=F32) + fqh - fkt[h:h + 1, 0:past]
            s_n = lax.dot_general(qx, kn2, nt_dims, preferred_element_type=F32) + fqh - fkt[h:h + 1, past:width]
            s_n = jnp.where(causal, s_n, MASK_VALUE)
            m = jnp.maximum(jnp.max(s_c, axis=-1, keepdims=True), jnp.max(s_n, axis=-1, keepdims=True))
            p_c = jnp.exp(s_c - m)
            p_n = jnp.exp(s_n - m)
            den = jnp.sum(p_c, axis=-1, keepdims=True) + jnp.sum(p_n, axis=-1, keepdims=True)
            o = jnp.dot(p_c.astype(BF16), vc2, preferred_element_type=F32)
            o = o + jnp.dot(p_n.astype(BF16), vn2, preferred_element_type=F32)
            outs.append(o / den)
        o2 = jnp.where(low, outs[0], outs[1])
        o_ref[0, :, cols] = (o2 * g_ref[0, :, cols].astype(F32)).astype(BF16)


def _fox_dec_call(q16, kc, vc, kn, vn, lft, lfc, ga, *, tn):
    bsz, past, _ = kc.shape
    npad = kn.shape[1]
    width = past + npad

    def per_b(shape):
        return pl.BlockSpec((1,) + shape, lambda b: (b, 0, 0))

    return pl.pallas_call(
        functools.partial(_fox_dec_kernel, past=past, tn=tn, npad=npad),
        grid=(bsz,),
        in_specs=[per_b((tn, D_ATTN)), per_b((past, D_ATTN)), per_b((past, D_ATTN)),
                  per_b((npad, D_ATTN)), per_b((npad, D_ATTN)),
                  per_b((N_HEADS, width)), per_b((width, LANES)), per_b((tn, D_ATTN))],
        out_specs=per_b((tn, D_ATTN)),
        out_shape=jax.ShapeDtypeStruct((bsz, tn, D_ATTN), BF16),
        compiler_params=pltpu.CompilerParams(dimension_semantics=("arbitrary",),
                                             vmem_limit_bytes=VMEM_LIMIT),
        name="fox_dec",
    )(q16, kc, vc, kn, vn, lft, lfc, ga)


def _out_kernel(ys_ref, ya_ref, x_ref, w_ref, g_ref, b_ref, o_ref, *, alpha):
    mixed = jnp.dot(ys_ref[...], w_ref[0:D_SSM, :], preferred_element_type=F32)
    mixed = mixed + jnp.dot(ya_ref[...], w_ref[D_SSM:D_SSM + D_ATTN, :], preferred_element_type=F32)
    h = alpha * x_ref[...] + mixed
    mu = jnp.mean(h, axis=-1, keepdims=True)
    hc = h - mu
    var = jnp.mean(hc * hc, axis=-1, keepdims=True)
    o_ref[...] = hc * lax.rsqrt(var + LN_EPS) * g_ref[...] + b_ref[...]


def _out_call(ys, ya, x, w16, g, b, *, alpha, tm):
    m = x.shape[0]
    assert m % tm == 0
    row = pl.BlockSpec((tm, D_MODEL), lambda i: (i, 0))

    def const(shape):
        return pl.BlockSpec(shape, lambda i: (0, 0), pipeline_mode=pl.Buffered(1))

    return pl.pallas_call(
        functools.partial(_out_kernel, alpha=alpha),
        grid=(m // tm,),
        in_specs=[row, row, row, const((D_SSM + D_ATTN, D_MODEL)), const((1, D_MODEL)), const((1, D_MODEL))],
        out_specs=row,
        out_shape=jax.ShapeDtypeStruct((m, D_MODEL), F32),
        compiler_params=pltpu.CompilerParams(dimension_semantics=("arbitrary",),
                                             vmem_limit_bytes=VMEM_LIMIT),
        name="out_proj",
    )(ys, ya, x, w16, g, b)


def _placement_constants():
    prep = np.zeros((LANES, N_HEADS * LANES), np.float32)
    pcd = np.zeros((LANES, D_SSM), np.float32)
    pext = np.zeros((LANES, 2 * D_ATTN), np.float32)
    for h in range(N_HEADS):
        for part in range(3):
            prep[16 * part + h, LANES * h:LANES * (h + 1)] = 1.0
            pcd[16 * part + h, HEAD_DIM * h:HEAD_DIM * (h + 1)] = 1.0
        base = LANES * (h // 2) + N_EXT * (h % 2)
        for part in range(3):
            pext[16 * part + h, base + part] = 1.0
            pext[SPLIT_ONE_LANE, base + 3 + part] = 1.0
            pext[SPLIT_ONE_LANE, D_ATTN + base + part] = 1.0
            pext[16 * part + h, D_ATTN + base + 3 + part] = -1.0
    return jnp.asarray(prep, BF16), jnp.asarray(pcd, BF16), jnp.asarray(pext, BF16)


def _permute_w_in(w):
    z_ssm, xbc = w[:, 0:1024], w[:, 1024:2560]
    dt, q, k, v = w[:, 2560:2576], w[:, 2576:3600], w[:, 3600:4624], w[:, 4624:5648]
    z_attn, f = w[:, 5648:6672], w[:, 6672:6688]
    pad = jnp.zeros((w.shape[0], LANES - 2 * N_HEADS), w.dtype)
    return jnp.concatenate([z_ssm, xbc, q, k, v, z_attn, dt, f, pad], axis=1).astype(BF16)


def _pad_rows(a, n, mode):
    return jnp.pad(a, ((0, 0), (0, n - a.shape[1]), (0, 0)), mode=mode)


def _state_to_kernel(h):
    return jnp.transpose(h, (0, 3, 1, 2)).reshape(h.shape[0], D_STATE, D_SSM)


def _state_from_kernel(ht):
    return jnp.transpose(ht.reshape(ht.shape[0], D_STATE, N_HEADS, HEAD_DIM), (0, 2, 3, 1))


def _layer(x, conv_state, ssm_state, past_k, past_v, past_logf, p, *, depth, prompt):
    bsz, tlen, _ = x.shape
    prep, pcd, pext = _placement_constants()
    alpha = (2 * depth) ** 0.25
    if prompt:
        tm = min(512, tlen)
        (k, v, qt, ka, vt, ga, gs, xbc, sm, csa, sct, smt, cout) = _proj_call(
            x, p["wp"], conv_state, p["cw"], p["cb"], p["sbias"], p["alog"], pext,
            tm=tm, chunk=SSD_CHUNK, aug=True)
        nsub = 4 if tlen % (4 * SSD_CHUNK) == 0 else 1
        y_ssm, ht = _ssd_call(xbc, gs, csa, sct, smt, _state_to_kernel(ssm_state), prep, pcd,
                              p["dskip"], p["nw"], nsub=nsub)
        y_attn = _fox_call(qt, ka, vt, ga, tq=min(2048, tlen), kb=4, ahead=4)
    else:
        assert tlen <= SSD_CHUNK
        (k, v, q16, ga, gs, xbc, sm, csa, sct, smt, cout) = _proj_call(
            x, p["wp"], conv_state, p["cw"], p["cb"], p["sbias"], p["alog"], None,
            tm=tlen, chunk=tlen, aug=False)
        y_ssm, ht = _ssd_call(
            _pad_rows(xbc, SSD_CHUNK, "constant"), _pad_rows(gs, SSD_CHUNK, "constant"),
            _pad_rows(csa, SSD_CHUNK, "edge"),
            jnp.pad(sct, ((0, 0), (0, 0), (0, SSD_CHUNK - tlen)), mode="edge"),
            jnp.pad(smt, ((0, 0), (0, 0), (0, SSD_CHUNK - tlen))),
            _state_to_kernel(ssm_state), prep, pcd, p["dskip"], p["nw"], nsub=1)
        y_ssm = y_ssm[:, :tlen]
        past = past_k.shape[1]
        lf_all = jnp.concatenate([past_logf, sm[:, :, 16:32],
                                  jnp.zeros((bsz, LANES - tlen, N_HEADS), F32)], axis=1)
        lft = jnp.transpose(lf_all, (0, 2, 1))
        lfc = jnp.pad(lf_all, ((0, 0), (0, 0), (0, LANES - N_HEADS)))
        y_attn = _fox_dec_call(q16, past_k.reshape(bsz, past, D_ATTN), past_v.reshape(bsz, past, D_ATTN),
                               _pad_rows(k, LANES, "constant"), _pad_rows(v, LANES, "constant"),
                               lft, lfc, ga, tn=tlen)
    m = bsz * tlen
    y = _out_call(y_ssm.reshape(m, D_SSM), y_attn.reshape(m, D_ATTN), x.reshape(m, D_MODEL),
                  p["wo"], p["ln_g"], p["ln_b"], alpha=alpha, tm=min(512, m))
    return (y.reshape(bsz, tlen, D_MODEL), cout, _state_from_kernel(ht),
            k.reshape(bsz, tlen, N_HEADS, HEAD_DIM), v.reshape(bsz, tlen, N_HEADS, HEAD_DIM),
            jnp.swapaxes(smt[:, 16:32, :], 1, 2))


def kernel(x_prompt, x_sample, cache_k, cache_v, cache_logf, state_conv, state_ssm, w_in, conv_w, conv_b,
           dt_bias, a_log, d_skip, ssm_norm_w, f_bias, w_out, ln_g, ln_b):
    depth = w_in.shape[0]
    yp, ys = x_prompt, x_sample
    bp = x_prompt.shape[0]
    outs_p, outs_s = [], []
    for layer in range(depth):
        zpad = jnp.zeros((1, LANES - 2 * N_HEADS), F32)
        p = dict(
            wp=_permute_w_in(w_in[layer]),
            cw=conv_w[layer], cb=conv_b[layer][None, :],
            sbias=jnp.concatenate([dt_bias[layer][None, :], f_bias[layer][None, :], zpad], axis=1),
            alog=jnp.concatenate([a_log[layer][None, :], jnp.zeros((1, LANES - N_HEADS), F32)], axis=1),
            dskip=jnp.repeat(d_skip[layer], HEAD_DIM)[None, :], nw=ssm_norm_w[layer][None, :],
            wo=w_out[layer].astype(BF16), ln_g=ln_g[layer][None, :], ln_b=ln_b[layer][None, :])
        conv0 = jnp.zeros((bp, CONV_W - 1, D_XBC), F32)
        ssm0 = jnp.zeros((bp, N_HEADS, HEAD_DIM, D_STATE), state_ssm.dtype)
        yp, cp, sp, kp, vp, fp = _layer(yp, conv0, ssm0, None, None, None, p, depth=depth, prompt=True)
        ys, cs, ss, kn, vn, fn = _layer(ys, state_conv[layer], state_ssm[layer], cache_k[layer], cache_v[layer],
                                        cache_logf[layer], p, depth=depth, prompt=False)
        outs_p.append((kp, vp, fp, cp, sp))
        outs_s.append((kn, vn, fn, cs, ss))

    def stack(items, idx):
        return jnp.stack([it[idx] for it in items])

    return (yp, ys, stack(outs_p, 0), stack(outs_p, 1), stack(outs_p, 2), stack(outs_p, 3), stack(outs_p, 4),
            stack(outs_s, 0), stack(outs_s, 1), stack(outs_s, 2), stack(outs_s, 3), stack(outs_s, 4))
```

```python
import functools

import numpy as np
import jax
import jax.numpy as jnp
from jax import lax
from jax.experimental import pallas as pl
from jax.experimental.pallas import tpu as pltpu

F32 = jnp.float32
BF16 = jnp.bfloat16

D_MODEL = 1024
D_SSM = 1024
D_ATTN = 1024
HEAD_DIM = 64
N_HEADS = 16
N_PAIRS = N_HEADS // 2
N_GROUPS = 2
D_STATE = 128
D_BC = N_GROUPS * D_STATE
CONV_W = 4
D_XBC = D_SSM + 2 * D_BC
ATTN_SCALE = HEAD_DIM ** -0.5
LN_EPS = 1e-5
RMS_EPS = 1e-5
LANES = 128
SSD_CHUNK = 128
MASK_VALUE = -1e30
LOG2E = 1.4426950408889634
FOX_TK = 256
VMEM_LIMIT = 56 * 1024 * 1024

C_ZS, C_XBC, C_Q, C_K, C_V, C_ZA, C_SM = 0, 1024, 2560, 3584, 4608, 5632, 6656
N_PROJ = C_SM + LANES
SPLIT_ONE_LANE = 3 * N_HEADS
N_EXT = 6


def _silu(z):
    return z * jax.nn.sigmoid(z)


def _split3_packed(val, lane, base):
    hi = val.astype(BF16).astype(F32)
    r1 = val - hi
    mid = r1.astype(BF16).astype(F32)
    lo = r1 - mid

    def to(v, dst):
        shift = (dst - base) % LANES
        return v if shift == 0 else pltpu.roll(v, shift, 1)

    return jnp.where(lane < 16, to(hi, 0),
                     jnp.where(lane < 32, to(mid, 16),
                               jnp.where(lane < 48, to(lo, 32), 0.0)))


def _proj_kernel(*refs, tm, chunk, aug):
    if aug:
        (x_ref, w_ref, cst_ref, cw_ref, cb_ref, sbias_ref, alog_ref, pext_ref,
         k_ref, v_ref, qt_ref, ka_ref, vt_ref, ga_ref, gs_ref, xbc_ref, sm_ref, csa_ref,
         sct_ref, smt_ref, cout_ref, xp_sc, fc_sc) = refs
    else:
        (x_ref, w_ref, cst_ref, cw_ref, cb_ref, sbias_ref, alog_ref,
         k_ref, v_ref, q16_ref, ga_ref, gs_ref, xbc_ref, sm_ref, csa_ref,
         sct_ref, smt_ref, cout_ref, xp_sc, fc_sc) = refs
    t = pl.program_id(1)
    nt = pl.num_programs(1)
    width = 2 * LANES

    @pl.when(t == 0)
    def _():
        fc_sc[...] = jnp.zeros_like(fc_sc)
        xp_sc[5:8, :] = cst_ref[0]

    @pl.when(t > 0)
    def _():
        xp_sc[5:8, :] = xp_sc[tm + 5:tm + 8, :]

    xb = x_ref[0].astype(BF16)
    lane = lax.broadcasted_iota(jnp.int32, (tm, LANES), 1)
    state = {}

    def post_small(res):
        row = lax.broadcasted_iota(jnp.int32, (tm, LANES), 0)
        u = res + sbias_ref[...]
        dt = jax.nn.softplus(u)
        logf = jax.nn.log_sigmoid(u)
        sm = jnp.where(lane < 16, dt, jnp.where(lane < 32, logf, 0.0))
        sm_ref[0] = sm
        aneg = -jnp.exp(alog_ref[...])
        s = jnp.where(lane < 16, dt * aneg, jnp.where(lane < 32, logf, 0.0))
        rowc = row & (chunk - 1)
        sh = 1
        while sh < tm:
            ok = (row >= sh) & ((lane >= 16) | (rowc >= sh))
            s = s + jnp.where(ok, pltpu.roll(s, sh, 0), 0.0)
            sh *= 2
        sc = s + fc_sc[...]
        lane1 = lax.broadcasted_iota(jnp.int32, (1, LANES), 1)
        fc_sc[...] = jnp.where((lane1 >= 16) & (lane1 < 32), sc[tm - 1:tm, :], 0.0)
        sct_ref[0] = sc.T
        smt_ref[0] = sm.T
        csa_ref[0] = _split3_packed(sc * LOG2E, lane, 0).astype(BF16)
        if aug:
            fa = _split3_packed(sc * LOG2E, lane, 16)
            state["fa"] = jnp.where(lane == SPLIT_ONE_LANE, 1.0, fa).astype(BF16)

    def post_xbc(c, res):
        xp_sc[8:8 + tm, width * c:width * (c + 1)] = res
        for j in range(2 * c, 2 * c + 2):
            sl = slice(LANES * j, LANES * (j + 1))
            acc = cb_ref[:, sl] + cw_ref[3:4, sl] * xp_sc[8:8 + tm, sl]
            acc = acc + cw_ref[2:3, sl] * xp_sc[7:7 + tm, sl]
            acc = acc + cw_ref[1:2, sl] * xp_sc[6:6 + tm, sl]
            acc = acc + cw_ref[0:1, sl] * xp_sc[5:5 + tm, sl]
            xbc_ref[0, :, sl] = _silu(acc).astype(BF16)

    def post_gate(ref, c, res):
        ref[0, :, width * c:width * (c + 1)] = _silu(res).astype(BF16)

    half = HEAD_DIM

    def post_k(c, res):
        k_ref[0, :, width * c:width * (c + 1)] = res
        if aug:
            for e in range(2):
                even = LANES * 2 * (2 * c + e)
                kcol = res[:, LANES * e:LANES * (e + 1)].astype(BF16)
                ka_ref[0, :, even:even + half] = kcol[:, 0:half]
                ka_ref[0, :, even + LANES + half:even + 2 * LANES] = kcol[:, half:LANES]

    def ext_k(c, res):
        for e in range(2):
            even = LANES * 2 * (2 * c + e)
            ext = res[:, LANES * e:LANES * (e + 1)]
            ka_ref[0, :, even + half:even + LANES] = pltpu.roll(ext, half, 1)[:, half:LANES].astype(BF16)
            ka_ref[0, :, even + LANES:even + LANES + half] = ext[:, 0:half].astype(BF16)

    def post_v(c, res):
        v_ref[0, :, width * c:width * (c + 1)] = res
        if aug:
            for r in range(tm // FOX_TK):
                vt_ref[0, r, width * c:width * (c + 1), :] = res[FOX_TK * r:FOX_TK * (r + 1), :].T.astype(BF16)

    def post_q(c, res):
        if aug:
            for e in range(2):
                qt = (res[:, LANES * e:LANES * (e + 1)] * (ATTN_SCALE * LOG2E)).T.astype(BF16)
                qt_ref[0, 2 * (2 * c + e), 0:half, :] = qt[0:half]
                qt_ref[0, 2 * (2 * c + e) + 1, half:LANES, :] = qt[half:LANES]
        else:
            q16_ref[0, :, width * c:width * (c + 1)] = (res * ATTN_SCALE).astype(BF16)

    def ext_q(c, res):
        rowt = lax.broadcasted_iota(jnp.int32, (half, tm), 0)
        for e in range(2):
            et = res[:, LANES * e:LANES * (e + 1)].T[0:half]
            qt_ref[0, 2 * (2 * c + e), half:LANES, :] = jnp.where(rowt < N_EXT, et, 0.0).astype(BF16)
            qt_ref[0, 2 * (2 * c + e) + 1, 0:half, :] = (
                jnp.where((rowt >= N_EXT) & (rowt < 2 * N_EXT), et, 0.0).astype(BF16))

    def post_ext(c, res):
        n_q = D_ATTN // width
        if c < n_q:
            ext_q(c, res)
        else:
            ext_k(c - n_q, res)

    def main_job(base, c, post):
        return (lambda: jnp.dot(xb, w_ref[:, base + width * c:base + width * (c + 1)],
                                preferred_element_type=F32), functools.partial(post, c))

    heavy = [main_job(C_XBC, c, post_xbc) for c in range(D_XBC // width)]
    heavy += [main_job(C_ZS, c, functools.partial(post_gate, gs_ref)) for c in range(D_SSM // width)]
    heavy += [main_job(C_ZA, c, functools.partial(post_gate, ga_ref)) for c in range(D_ATTN // width)]
    light = [main_job(C_K, c, post_k) for c in range(D_ATTN // width)]
    light += [main_job(C_V, c, post_v) for c in range(D_ATTN // width)]
    light += [main_job(C_Q, c, post_q) for c in range(D_ATTN // width)]
    if aug:
        light += [(lambda c=c: jnp.dot(state["fa"], pext_ref[:, width * c:width * (c + 1)],
                                       preferred_element_type=F32), functools.partial(post_ext, c))
                  for c in range(2 * D_ATTN // width)]
    jobs = [(lambda: jnp.dot(xb, w_ref[:, C_SM:N_PROJ], preferred_element_type=F32), post_small)]
    while heavy or light:
        jobs += light[:1] + heavy[:1]
        light, heavy = light[1:], heavy[1:]
    ahead = 2
    pending = [jobs[n][0]() for n in range(ahead)]
    for n, (_, post) in enumerate(jobs):
        res = pending.pop(0)
        if n + ahead < len(jobs):
            pending.append(jobs[n + ahead][0]())
        post(res)

    @pl.when(t == nt - 1)
    def _():
        cout_ref[0] = xp_sc[tm + 5:tm + 8, :]


def _const_spec(shape):
    return pl.BlockSpec(shape, lambda b, t: (0,) * len(shape), pipeline_mode=pl.Buffered(1))


def _proj_call(x, wp, cst, cw, cb, sbias, alog, pext, *, tm, chunk, aug):
    bsz, tlen, _ = x.shape
    assert tlen % tm == 0 and tm % chunk == 0 and chunk & (chunk - 1) == 0
    grid = (bsz, tlen // tm)

    def tok(width):
        return pl.BlockSpec((1, tm, width), lambda b, t: (b, t, 0))

    tr = pl.BlockSpec((1, LANES, tm), lambda b, t: (b, 0, t))
    per_b = pl.BlockSpec((1, CONV_W - 1, D_XBC), lambda b, t: (b, 0, 0))
    in_specs = [tok(D_MODEL), _const_spec((D_MODEL, N_PROJ)), per_b, _const_spec((CONV_W, D_XBC)),
                _const_spec((1, D_XBC)), _const_spec((1, LANES)), _const_spec((1, LANES))]
    args = [x, wp, cst, cw, cb, sbias, alog]

    def sds(width, dtype):
        return jax.ShapeDtypeStruct((bsz, tlen, width), dtype)

    out_shape = [sds(D_ATTN, F32), sds(D_ATTN, F32)]
    out_specs = [tok(D_ATTN), tok(D_ATTN)]
    if aug:
        in_specs.append(_const_spec((LANES, 2 * D_ATTN)))
        args.append(pext)
        assert tm % FOX_TK == 0
        out_shape += [jax.ShapeDtypeStruct((bsz, N_HEADS, LANES, tlen), BF16), sds(2 * D_ATTN, BF16),
                      jax.ShapeDtypeStruct((bsz, tlen // FOX_TK, D_ATTN, FOX_TK), BF16)]
        out_specs += [pl.BlockSpec((1, N_HEADS, LANES, tm), lambda b, t: (b, 0, 0, t)), tok(2 * D_ATTN),
                      pl.BlockSpec((1, tm // FOX_TK, D_ATTN, FOX_TK), lambda b, t: (b, t, 0, 0))]
    else:
        out_shape += [sds(D_ATTN, BF16)]
        out_specs += [tok(D_ATTN)]
    out_shape += [sds(D_ATTN, BF16), sds(D_SSM, BF16), sds(D_XBC, BF16), sds(LANES, F32), sds(LANES, BF16),
                  jax.ShapeDtypeStruct((bsz, LANES, tlen), F32), jax.ShapeDtypeStruct((bsz, LANES, tlen), F32),
                  jax.ShapeDtypeStruct((bsz, CONV_W - 1, D_XBC), F32)]
    out_specs += [tok(D_ATTN), tok(D_SSM), tok(D_XBC), tok(LANES), tok(LANES), tr, tr, per_b]
    return pl.pallas_call(
        functools.partial(_proj_kernel, tm=tm, chunk=chunk, aug=aug),
        grid=grid, in_specs=in_specs, out_specs=out_specs, out_shape=out_shape,
        scratch_shapes=[pltpu.VMEM((tm + 8, D_XBC), F32), pltpu.VMEM((1, LANES), F32)],
        compiler_params=pltpu.CompilerParams(dimension_semantics=("arbitrary", "arbitrary"),
                                             vmem_limit_bytes=VMEM_LIMIT),
        name="proj_aug" if aug else "proj_dec",
    )(*args)


def _ssd_kernel(xbc_ref, gate_ref, csa_ref, sct_ref, smt_ref, h0_ref, prep_ref, pcd_ref, dskip_ref, nw_ref,
                y_ref, hout_ref, h_sc, y_sc, *, nsub):
    L = SSD_CHUNK
    t = pl.program_id(1)
    nt = pl.num_programs(1)

    @pl.when(t == 0)
    def _():
        h_sc[...] = h0_ref[0]

    lane = lax.broadcasted_iota(jnp.int32, (L, LANES), 1)
    low = lane < HEAD_DIM
    tri = lax.broadcasted_iota(jnp.int32, (L, L), 0) >= lax.broadcasted_iota(jnp.int32, (L, L), 1)
    zero16 = jnp.zeros((L, LANES), BF16)

    for j in range(nsub):
        rows = slice(j * L, (j + 1) * L)
        csa = csa_ref[0, rows, :]
        csrep = jnp.dot(csa, prep_ref[...], preferred_element_type=F32)
        cst = sct_ref[0, 0:N_HEADS, rows] * LOG2E
        dtt = smt_ref[0, 0:N_HEADS, rows]
        w2t = dtt * jnp.exp2(cst[:, L - 1:L] - cst)
        cst_dt = cst - jnp.log2(dtt)
        cd = jnp.exp2(jnp.dot(csa[L - 8:L, :], pcd_ref[...], preferred_element_type=F32)[7:8, :])
        for g in range(N_GROUPS):
            bg = xbc_ref[0, rows, D_SSM + D_STATE * g:D_SSM + D_STATE * (g + 1)]
            cg = xbc_ref[0, rows, D_SSM + D_BC + D_STATE * g:D_SSM + D_BC + D_STATE * (g + 1)]
            cb = lax.dot_general(cg, bg, (((1,), (1,)), ((), ())), preferred_element_type=F32)
            bgt = bg.astype(F32).T
            cgf = cg.astype(F32)

            def head_parts(h):
                rep = csrep[:, LANES * h:LANES * (h + 1)]
                dec = jnp.exp2(jnp.where(tri, rep - cst_dt[h:h + 1, :], -jnp.inf))
                m = (cb * dec).astype(BF16)
                ce = (cgf * jnp.exp2(rep)).astype(BF16)
                btw = (bgt * w2t[h:h + 1, :]).astype(BF16)
                return m, ce, btw

            for cc in range(N_PAIRS // N_GROUPS):
                c = (N_PAIRS // N_GROUPS) * g + cc
                cols = slice(LANES * c, LANES * (c + 1))
                xs = xbc_ref[0, rows, cols]
                xa = jnp.where(low, xs, zero16)
                xb = jnp.where(low, zero16, xs)
                hp = h_sc[:, cols]
                hp16 = hp.astype(BF16)
                ha = jnp.where(low, hp16, zero16)
                hb = jnp.where(low, zero16, hp16)
                ma, cea, btwa = head_parts(2 * c)
                mb, ceb, btwb = head_parts(2 * c + 1)
                y = jnp.dot(jnp.concatenate([ma, cea], axis=1), jnp.concatenate([xa, ha], axis=0),
                            preferred_element_type=F32)
                y = y + jnp.dot(jnp.concatenate([mb, ceb], axis=1), jnp.concatenate([xb, hb], axis=0),
                                preferred_element_type=F32)
                st = jnp.dot(jnp.concatenate([btwa, btwb], axis=1), jnp.concatenate([xa, xb], axis=0),
                             preferred_element_type=F32)
                h_sc[:, cols] = hp * cd[:, cols] + st
                y_sc[rows, cols] = y + dskip_ref[:, cols] * xs.astype(F32)

    half = D_SSM // N_GROUPS
    for g in range(N_GROUPS):
        cols = slice(half * g, half * (g + 1))
        yg = y_sc[:, cols] * gate_ref[0, :, cols].astype(F32)
        ms = jnp.mean(yg * yg, axis=-1, keepdims=True)
        y_ref[0, :, cols] = (yg * lax.rsqrt(ms + RMS_EPS) * nw_ref[:, cols]).astype(BF16)

    @pl.when(t == nt - 1)
    def _():
        hout_ref[0] = h_sc[...]


def _ssd_call(xbc, gate, csa, sct, smt, h0t, prep, pcd, dskip, nw, *, nsub):
    bsz, tlen, _ = xbc.shape
    tmc = SSD_CHUNK * nsub
    assert tlen % tmc == 0
    grid = (bsz, tlen // tmc)

    def tok(width):
        return pl.BlockSpec((1, tmc, width), lambda b, t: (b, t, 0))

    tr = pl.BlockSpec((1, LANES, tmc), lambda b, t: (b, 0, t))
    st = pl.BlockSpec((1, D_STATE, D_SSM), lambda b, t: (b, 0, 0))
    return pl.pallas_call(
        functools.partial(_ssd_kernel, nsub=nsub),
        grid=grid,
        in_specs=[tok(D_XBC), tok(D_SSM), tok(LANES), tr, tr, st,
                  _const_spec((LANES, N_HEADS * LANES)), _const_spec((LANES, D_SSM)),
                  _const_spec((1, D_SSM)), _const_spec((1, D_SSM))],
        out_specs=[tok(D_SSM), st],
        out_shape=[jax.ShapeDtypeStruct((bsz, tlen, D_SSM), BF16),
                   jax.ShapeDtypeStruct((bsz, D_STATE, D_SSM), F32)],
        scratch_shapes=[pltpu.VMEM((D_STATE, D_SSM), F32), pltpu.VMEM((tmc, D_SSM), F32)],
        compiler_params=pltpu.CompilerParams(dimension_semantics=("arbitrary", "arbitrary"),
                                             vmem_limit_bytes=VMEM_LIMIT),
        name="ssd",
    )(xbc, gate, csa, sct, smt, h0t, prep, pcd, dskip, nw)


def _fox_kernel(qt_ref, ka_ref, vt_ref, g_ref, o_ref, m_sc, l_sc, acc_sc, st_sc, *, tq, kb, ahead):
    tk = FOX_TK
    i = pl.program_id(2)
    nslab = tq // tk
    assert nslab % kb == 0
    m_sc[...] = jnp.full_like(m_sc, MASK_VALUE)
    l_sc[...] = jnp.zeros_like(l_sc)
    acc_sc[...] = jnp.zeros_like(acc_sc)
    visible = lax.broadcasted_iota(jnp.int32, (tk, tk), 0) <= lax.broadcasted_iota(jnp.int32, (tk, tk), 1)
    ones_rows = jnp.ones((16, tk), BF16)

    def scores(unit):
        j, x, slab, _ = unit
        kblk = ka_ref[0, pl.ds(pl.multiple_of(j * tk, tk), tk), LANES * x:LANES * (x + 1)]
        return jnp.dot(kblk, qt_ref[0, x, :, tk * slab:tk * (slab + 1)], preferred_element_type=F32)

    def finish(unit, st):
        j, x, slab, masked = unit
        cols = slice(tk * slab, tk * (slab + 1))
        if masked:
            st = jnp.where(visible, st, MASK_VALUE)
        m_old = m_sc[x, :, cols]
        m_new = jnp.maximum(m_old, jnp.max(st, axis=0, keepdims=True))
        p = jnp.exp2(st - m_new).astype(BF16)
        alpha = jnp.exp2(m_old - m_new)
        vt1 = jnp.concatenate([vt_ref[0, j, HEAD_DIM * x:HEAD_DIM * (x + 1), :], ones_rows], axis=0)
        pv = jnp.dot(vt1, p, preferred_element_type=F32)
        l_sc[x, :, cols] = alpha * l_sc[x, :, cols] + pv[HEAD_DIM:HEAD_DIM + 1, :]
        acc_sc[x, :, cols] = alpha * acc_sc[x, :, cols] + pv[0:HEAD_DIM, :]
        m_sc[x, :, cols] = m_new

    def run(units, next_units):
        pending = []
        for n, unit in enumerate(units):
            st = st_sc[n] if n < ahead else pending.pop(0)
            if n + ahead < len(units):
                pending.append(scores(units[n + ahead]))
            elif next_units is not None:
                st_sc[n + ahead - len(units)] = scores(next_units[n + ahead - len(units)])
            finish(unit, st)

    def full_units(jo):
        return [(kb * jo + jb, x, slab, False) for jb in range(kb) for x in range(2) for slab in range(nslab)]

    tail_units = [(nslab * i + jj, x, slab, slab == jj)
                  for jj in range(nslab) for x in range(2) for slab in range(jj, nslab)]
    assert all(f[1:3] == t[1:3] for f, t in zip(full_units(0)[:ahead], tail_units[:ahead]))
    assert ahead <= 2 * nslab
    for a, unit in enumerate(full_units(0)[:ahead]):
        st_sc[a] = scores(unit)

    def body(jo, carry):
        run(full_units(jo), full_units(jo + 1))
        return carry

    lax.fori_loop(0, (nslab * i) // kb, body, 0)
    run(tail_units, None)
    ot = jnp.concatenate([acc_sc[0] / l_sc[0], acc_sc[1] / l_sc[1]], axis=0)
    o_ref[0] = (ot.T * g_ref[0].astype(F32)).astype(BF16)


def _fox_call(qt, ka, vt, ga, *, tq, kb, ahead):
    bsz, nkv, _, tk = vt.shape
    tlen = nkv * tk
    assert tlen % tq == 0 and tq % tk == 0 and tk == FOX_TK
    grid = (bsz, N_PAIRS, tlen // tq)
    return pl.pallas_call(
        functools.partial(_fox_kernel, tq=tq, kb=kb, ahead=ahead),
        grid=grid,
        in_specs=[pl.BlockSpec((1, 2, LANES, tq), lambda b, c, i: (b, c, 0, i)),
                  pl.BlockSpec((1, tlen, 2 * LANES), lambda b, c, i: (b, 0, c)),
                  pl.BlockSpec((1, nkv, LANES, tk), lambda b, c, i: (b, 0, c, 0)),
                  pl.BlockSpec((1, tq, LANES), lambda b, c, i: (b, i, c))],
        out_specs=pl.BlockSpec((1, tq, LANES), lambda b, c, i: (b, i, c)),
        out_shape=jax.ShapeDtypeStruct((bsz, tlen, D_ATTN), BF16),
        scratch_shapes=[pltpu.VMEM((2, 1, tq), F32), pltpu.VMEM((2, 1, tq), F32),
                        pltpu.VMEM((2, HEAD_DIM, tq), F32), pltpu.VMEM((ahead, tk, tk), F32)],
        compiler_params=pltpu.CompilerParams(dimension_semantics=("arbitrary", "arbitrary", "arbitrary"),
                                             vmem_limit_bytes=VMEM_LIMIT),
        name="fox_prompt",
    )(qt, ka, vt, ga)


def _fox_dec_kernel(q_ref, kc_ref, vc_ref, kn_ref, vn_ref, lft_ref, lfc_ref, g_ref, o_ref, *, past, tn, npad):
    width = past + npad
    fkt = lft_ref[0]
    lane_w = lax.broadcasted_iota(jnp.int32, fkt.shape, 1)
    sh = 1
    while sh < width:
        fkt = fkt + jnp.where(lane_w >= sh, pltpu.roll(fkt, sh, 1), 0.0)
        sh *= 2
    fcol = lfc_ref[0]
    row_w = lax.broadcasted_iota(jnp.int32, fcol.shape, 0)
    sh = 1
    while sh < width:
        fcol = fcol + jnp.where(row_w >= sh, pltpu.roll(fcol, sh, 0), 0.0)
        sh *= 2
    fq = fcol[past:past + tn, :]
    lane = lax.broadcasted_iota(jnp.int32, (tn, LANES), 1)
    low = lane < HEAD_DIM
    causal = lax.broadcasted_iota(jnp.int32, (tn, npad), 1) <= lax.broadcasted_iota(jnp.int32, (tn, npad), 0)
    nt_dims = (((1,), (1,)), ((), ()))
    for c in range(N_PAIRS):
        cols = slice(LANES * c, LANES * (c + 1))
        q2 = q_ref[0, :, cols]
        zq = jnp.zeros_like(q2)
        kc2 = kc_ref[0, :, cols].astype(BF16)
        vc2 = vc_ref[0, :, cols].astype(BF16)
        kn2 = kn_ref[0, :, cols].astype(BF16)
        vn2 = vn_ref[0, :, cols].astype(BF16)
        outs = []
        for x in range(2):
            h = 2 * c + x
            qx = jnp.where(low, q2, zq) if x == 0 else jnp.where(low, zq, q2)
            fqh = fq[:, h:h + 1]
            s_c = lax.dot_general(qx, kc2, nt_dims, preferred_element_type=F32) + fqh - fkt[h:h + 1, 0:past]
            s_n = lax.dot_general(qx, kn2, nt_dims, preferred_element_type=F32) + fqh - fkt[h:h + 1, past:width]
            s_n = jnp.where(causal, s_n, MASK_VALUE)
            m = jnp.maximum(jnp.max(s_c, axis=-1, keepdims=True), jnp.max(s_n, axis=-1, keepdims=True))
            p_c = jnp.exp(s_c - m)
            p_n = jnp.exp(s_n - m)
            den = jnp.sum(p_c, axis=-1, keepdims=True) + jnp.sum(p_n, axis=-1, keepdims=True)
            o = jnp.dot(p_c.astype(BF16), vc2, preferred_element_type=F32)
            o = o + jnp.dot(p_n.astype(BF16), vn2, preferred_element_type=F32)
            outs.append(o / den)
        o2 = jnp.where(low, outs[0], outs[1])
        o_ref[0, :, cols] = (o2 * g_ref[0, :, cols].astype(F32)).astype(BF16)


def _fox_dec_call(q16, kc, vc, kn, vn, lft, lfc, ga, *, tn):
    bsz, past, _ = kc.shape
    npad = kn.shape[1]
    width = past + npad

    def per_b(shape):
        return pl.BlockSpec((1,) + shape, lambda b: (b, 0, 0))

    return pl.pallas_call(
        functools.partial(_fox_dec_kernel, past=past, tn=tn, npad=npad),
        grid=(bsz,),
        in_specs=[per_b((tn, D_ATTN)), per_b((past, D_ATTN)), per_b((past, D_ATTN)),
                  per_b((npad, D_ATTN)), per_b((npad, D_ATTN)),
                  per_b((N_HEADS, width)), per_b((width, LANES)), per_b((tn, D_ATTN))],
        out_specs=per_b((tn, D_ATTN)),
        out_shape=jax.ShapeDtypeStruct((bsz, tn, D_ATTN), BF16),
        compiler_params=pltpu.CompilerParams(dimension_semantics=("arbitrary",),
                                             vmem_limit_bytes=VMEM_LIMIT),
        name="fox_dec",
    )(q16, kc, vc, kn, vn, lft, lfc, ga)


def _out_kernel(ys_ref, ya_ref, x_ref, w_ref, g_ref, b_ref, o_ref, *, alpha):
    mixed = jnp.dot(ys_ref[...], w_ref[0:D_SSM, :], preferred_element_type=F32)
    mixed = mixed + jnp.dot(ya_ref[...], w_ref[D_SSM:D_SSM + D_ATTN, :], preferred_element_type=F32)
    h = alpha * x_ref[...] + mixed
    mu = jnp.mean(h, axis=-1, keepdims=True)
    hc = h - mu
    var = jnp.mean(hc * hc, axis=-1, keepdims=True)
    o_ref[...] = hc * lax.rsqrt(var + LN_EPS) * g_ref[...] + b_ref[...]


def _out_call(ys, ya, x, w16, g, b, *, alpha, tm):
    m = x.shape[0]
    assert m % tm == 0
    row = pl.BlockSpec((tm, D_MODEL), lambda i: (i, 0))

    def const(shape):
        return pl.BlockSpec(shape, lambda i: (0, 0), pipeline_mode=pl.Buffered(1))

    return pl.pallas_call(
        functools.partial(_out_kernel, alpha=alpha),
        grid=(m // tm,),
        in_specs=[row, row, row, const((D_SSM + D_ATTN, D_MODEL)), const((1, D_MODEL)), const((1, D_MODEL))],
        out_specs=row,
        out_shape=jax.ShapeDtypeStruct((m, D_MODEL), F32),
        compiler_params=pltpu.CompilerParams(dimension_semantics=("arbitrary",),
                                             vmem_limit_bytes=VMEM_LIMIT),
        name="out_proj",
    )(ys, ya, x, w16, g, b)


def _placement_constants():
    prep = np.zeros((LANES, N_HEADS * LANES), np.float32)
    pcd = np.zeros((LANES, D_SSM), np.float32)
    pext = np.zeros((LANES, 2 * D_ATTN), np.float32)
    for h in range(N_HEADS):
        for part in range(3):
            prep[16 * part + h, LANES * h:LANES * (h + 1)] = 1.0
            pcd[16 * part + h, HEAD_DIM * h:HEAD_DIM * (h + 1)] = 1.0
        base = LANES * (h // 2) + N_EXT * (h % 2)
        for part in range(3):
            pext[16 * part + h, base + part] = 1.0
            pext[SPLIT_ONE_LANE, base + 3 + part] = 1.0
            pext[SPLIT_ONE_LANE, D_ATTN + base + part] = 1.0
            pext[16 * part + h, D_ATTN + base + 3 + part] = -1.0
    return jnp.asarray(prep, BF16), jnp.asarray(pcd, BF16), jnp.asarray(pext, BF16)


def _permute_w_in(w):
    z_ssm, xbc = w[:, 0:1024], w[:, 1024:2560]
    dt, q, k, v = w[:, 2560:2576], w[:, 2576:3600], w[:, 3600:4624], w[:, 4624:5648]
    z_attn, f = w[:, 5648:6672], w[:, 6672:6688]
    pad = jnp.zeros((w.shape[0], LANES - 2 * N_HEADS), w.dtype)
    return jnp.concatenate([z_ssm, xbc, q, k, v, z_attn, dt, f, pad], axis=1).astype(BF16)


def _pad_rows(a, n, mode):
    return jnp.pad(a, ((0, 0), (0, n - a.shape[1]), (0, 0)), mode=mode)


def _state_to_kernel(h):
    return jnp.transpose(h, (0, 3, 1, 2)).reshape(h.shape[0], D_STATE, D_SSM)


def _state_from_kernel(ht):
    return jnp.transpose(ht.reshape(ht.shape[0], D_STATE, N_HEADS, HEAD_DIM), (0, 2, 3, 1))


def _layer(x, conv_state, ssm_state, past_k, past_v, past_logf, p, *, depth, prompt):
    bsz, tlen, _ = x.shape
    prep, pcd, pext = _placement_constants()
    alpha = (2 * depth) ** 0.25
    if prompt:
        tm = min(512, tlen)
        (k, v, qt, ka, vt, ga, gs, xbc, sm, csa, sct, smt, cout) = _proj_call(
            x, p["wp"], conv_state, p["cw"], p["cb"], p["sbias"], p["alog"], pext,
            tm=tm, chunk=SSD_CHUNK, aug=True)
        nsub = 4 if tlen % (4 * SSD_CHUNK) == 0 else 1
        y_ssm, ht = _ssd_call(xbc, gs, csa, sct, smt, _state_to_kernel(ssm_state), prep, pcd,
                              p["dskip"], p["nw"], nsub=nsub)
        y_attn = _fox_call(qt, ka, vt, ga, tq=min(2048, tlen), kb=4, ahead=5)
    else:
        assert tlen <= SSD_CHUNK
        (k, v, q16, ga, gs, xbc, sm, csa, sct, smt, cout) = _proj_call(
            x, p["wp"], conv_state, p["cw"], p["cb"], p["sbias"], p["alog"], None,
            tm=tlen, chunk=tlen, aug=False)
        y_ssm, ht = _ssd_call(
            _pad_rows(xbc, SSD_CHUNK, "constant"), _pad_rows(gs, SSD_CHUNK, "constant"),
            _pad_rows(csa, SSD_CHUNK, "edge"),
            jnp.pad(sct, ((0, 0), (0, 0), (0, SSD_CHUNK - tlen)), mode="edge"),
            jnp.pad(smt, ((0, 0), (0, 0), (0, SSD_CHUNK - tlen))),
            _state_to_kernel(ssm_state), prep, pcd, p["dskip"], p["nw"], nsub=1)
        y_ssm = y_ssm[:, :tlen]
        past = past_k.shape[1]
        lf_all = jnp.concatenate([past_logf, sm[:, :, 16:32],
                                  jnp.zeros((bsz, LANES - tlen, N_HEADS), F32)], axis=1)
        lft = jnp.transpose(lf_all, (0, 2, 1))
        lfc = jnp.pad(lf_all, ((0, 0), (0, 0), (0, LANES - N_HEADS)))
        y_attn = _fox_dec_call(q16, past_k.reshape(bsz, past, D_ATTN), past_v.reshape(bsz, past, D_ATTN),
                               _pad_rows(k, LANES, "constant"), _pad_rows(v, LANES, "constant"),
                               lft, lfc, ga, tn=tlen)
    m = bsz * tlen
    y = _out_call(y_ssm.reshape(m, D_SSM), y_attn.reshape(m, D_ATTN), x.reshape(m, D_MODEL),
                  p["wo"], p["ln_g"], p["ln_b"], alpha=alpha, tm=min(512, m))
    return (y.reshape(bsz, tlen, D_MODEL), cout, _state_from_kernel(ht),
            k.reshape(bsz, tlen, N_HEADS, HEAD_DIM), v.reshape(bsz, tlen, N_HEADS, HEAD_DIM),
            jnp.swapaxes(smt[:, 16:32, :], 1, 2))


def kernel(x_prompt, x_sample, cache_k, cache_v, cache_logf, state_conv, state_ssm, w_in, conv_w, conv_b,
           dt_bias, a_log, d_skip, ssm_norm_w, f_bias, w_out, ln_g, ln_b):
    depth = w_in.shape[0]
    yp, ys = x_prompt, x_sample
    bp = x_prompt.shape[0]
    outs_p, outs_s = [], []
    for layer in range(depth):
        zpad = jnp.zeros((1, LANES - 2 * N_HEADS), F32)
        p = dict(
            wp=_permute_w_in(w_in[layer]),
            cw=conv_w[layer], cb=conv_b[layer][None, :],
            sbias=jnp.concatenate([dt_bias[layer][None, :], f_bias[layer][None, :], zpad], axis=1),
            alog=jnp.concatenate([a_log[layer][None, :], jnp.zeros((1, LANES - N_HEADS), F32)], axis=1),
            dskip=jnp.repeat(d_skip[layer], HEAD_DIM)[None, :], nw=ssm_norm_w[layer][None, :],
            wo=w_out[layer].astype(BF16), ln_g=ln_g[layer][None, :], ln_b=ln_b[layer][None, :])
        conv0 = jnp.zeros((bp, CONV_W - 1, D_XBC), F32)
        ssm0 = jnp.zeros((bp, N_HEADS, HEAD_DIM, D_STATE), state_ssm.dtype)
        yp, cp, sp, kp, vp, fp = _layer(yp, conv0, ssm0, None, None, None, p, depth=depth, prompt=True)
        ys, cs, ss, kn, vn, fn = _layer(ys, state_conv[layer], state_ssm[layer], cache_k[layer], cache_v[layer],
                                        cache_logf[layer], p, depth=depth, prompt=False)
        outs_p.append((kp, vp, fp, cp, sp))
        outs_s.append((kn, vn, fn, cs, ss))

    def stack(items, idx):
        return jnp.stack([it[idx] for it in items])

    return (yp, ys, stack(outs_p, 0), stack(outs_p, 1), stack(outs_p, 2), stack(outs_p, 3), stack(outs_p, 4),
            stack(outs_s, 0), stack(outs_s, 1), stack(outs_s, 2), stack(outs_s, 3), stack(outs_s, 4))
```

```python
import functools

import numpy as np
import jax
import jax.numpy as jnp
from jax import lax
from jax.experimental import pallas as pl
from jax.experimental.pallas import tpu as pltpu

F32 = jnp.float32
BF16 = jnp.bfloat16

D_MODEL = 1024
D_SSM = 1024
D_ATTN = 1024
HEAD_DIM = 64
N_HEADS = 16
N_PAIRS = N_HEADS // 2
N_GROUPS = 2
D_STATE = 128
D_BC = N_GROUPS * D_STATE
CONV_W = 4
D_XBC = D_SSM + 2 * D_BC
ATTN_SCALE = HEAD_DIM ** -0.5
LN_EPS = 1e-5
RMS_EPS = 1e-5
LANES = 128
SSD_CHUNK = 128
MASK_VALUE = -1e30
LOG2E = 1.4426950408889634
FOX_TK = 256
PROJ_TM = 512
SSD_NSUB = 4
FOX_TQ = 2048
FOX_KB = 8
FOX_AHEAD = 5
OUT_TM = 1024
VMEM_LIMIT = 56 * 1024 * 1024

C_ZS, C_XBC, C_Q, C_K, C_V, C_ZA, C_SM = 0, 1024, 2560, 3584, 4608, 5632, 6656
N_PROJ = C_SM + LANES
SPLIT_ONE_LANE = 3 * N_HEADS
N_EXT = 6


def _silu(z):
    return z * jax.nn.sigmoid(z)


def _split3_packed(val, lane, base):
    hi = val.astype(BF16).astype(F32)
    r1 = val - hi
    mid = r1.astype(BF16).astype(F32)
    lo = r1 - mid

    def to(v, dst):
        shift = (dst - base) % LANES
        return v if shift == 0 else pltpu.roll(v, shift, 1)

    return jnp.where(lane < 16, to(hi, 0),
                     jnp.where(lane < 32, to(mid, 16),
                               jnp.where(lane < 48, to(lo, 32), 0.0)))


def _proj_kernel(*refs, tm, chunk, aug):
    if aug:
        (x_ref, w_ref, cst_ref, cw_ref, cb_ref, sbias_ref, alog_ref, pext_ref,
         k_ref, v_ref, qt_ref, ka_ref, vt_ref, ga_ref, gs_ref, xbc_ref, sm_ref, csa_ref,
         sct_ref, smt_ref, cout_ref, xp_sc, fc_sc) = refs
    else:
        (x_ref, w_ref, cst_ref, cw_ref, cb_ref, sbias_ref, alog_ref,
         k_ref, v_ref, q16_ref, ga_ref, gs_ref, xbc_ref, sm_ref, csa_ref,
         sct_ref, smt_ref, cout_ref, xp_sc, fc_sc) = refs
    t = pl.program_id(1)
    nt = pl.num_programs(1)
    width = 2 * LANES

    @pl.when(t == 0)
    def _():
        fc_sc[...] = jnp.zeros_like(fc_sc)
        xp_sc[5:8, :] = cst_ref[0]

    @pl.when(t > 0)
    def _():
        xp_sc[5:8, :] = xp_sc[tm + 5:tm + 8, :]

    xb = x_ref[0].astype(BF16)
    lane = lax.broadcasted_iota(jnp.int32, (tm, LANES), 1)
    state = {}

    def post_small(res):
        row = lax.broadcasted_iota(jnp.int32, (tm, LANES), 0)
        u = res + sbias_ref[...]
        dt = jax.nn.softplus(u)
        logf = jax.nn.log_sigmoid(u)
        sm = jnp.where(lane < 16, dt, jnp.where(lane < 32, logf, 0.0))
        sm_ref[0] = sm
        aneg = -jnp.exp(alog_ref[...])
        s = jnp.where(lane < 16, dt * aneg, jnp.where(lane < 32, logf, 0.0))
        rowc = row & (chunk - 1)
        sh = 1
        while sh < tm:
            ok = (row >= sh) & ((lane >= 16) | (rowc >= sh))
            s = s + jnp.where(ok, pltpu.roll(s, sh, 0), 0.0)
            sh *= 2
        sc = s + fc_sc[...]
        lane1 = lax.broadcasted_iota(jnp.int32, (1, LANES), 1)
        fc_sc[...] = jnp.where((lane1 >= 16) & (lane1 < 32), sc[tm - 1:tm, :], 0.0)
        sct_ref[0] = sc.T
        smt_ref[0] = sm.T
        csa_ref[0] = _split3_packed(sc * LOG2E, lane, 0).astype(BF16)
        if aug:
            fa = _split3_packed(sc * LOG2E, lane, 16)
            state["fa"] = jnp.where(lane == SPLIT_ONE_LANE, 1.0, fa).astype(BF16)

    def post_xbc(c, res):
        xp_sc[8:8 + tm, width * c:width * (c + 1)] = res
        for j in range(2 * c, 2 * c + 2):
            sl = slice(LANES * j, LANES * (j + 1))
            acc = cb_ref[:, sl] + cw_ref[3:4, sl] * xp_sc[8:8 + tm, sl]
            acc = acc + cw_ref[2:3, sl] * xp_sc[7:7 + tm, sl]
            acc = acc + cw_ref[1:2, sl] * xp_sc[6:6 + tm, sl]
            acc = acc + cw_ref[0:1, sl] * xp_sc[5:5 + tm, sl]
            xbc_ref[0, :, sl] = _silu(acc).astype(BF16)

    def post_gate(ref, c, res):
        ref[0, :, width * c:width * (c + 1)] = _silu(res).astype(BF16)

    half = HEAD_DIM

    def post_k(c, res):
        k_ref[0, :, width * c:width * (c + 1)] = res
        if aug:
            for e in range(2):
                even = LANES * 2 * (2 * c + e)
                kcol = res[:, LANES * e:LANES * (e + 1)].astype(BF16)
                ka_ref[0, :, even:even + half] = kcol[:, 0:half]
                ka_ref[0, :, even + LANES + half:even + 2 * LANES] = kcol[:, half:LANES]

    def ext_k(c, res):
        for e in range(2):
            even = LANES * 2 * (2 * c + e)
            ext = res[:, LANES * e:LANES * (e + 1)]
            ka_ref[0, :, even + half:even + LANES] = pltpu.roll(ext, half, 1)[:, half:LANES].astype(BF16)
            ka_ref[0, :, even + LANES:even + LANES + half] = ext[:, 0:half].astype(BF16)

    def post_v(c, res):
        v_ref[0, :, width * c:width * (c + 1)] = res
        if aug:
            for r in range(tm // FOX_TK):
                vt_ref[0, r, width * c:width * (c + 1), :] = res[FOX_TK * r:FOX_TK * (r + 1), :].T.astype(BF16)

    def post_q(c, res):
        if aug:
            for e in range(2):
                qt = (res[:, LANES * e:LANES * (e + 1)] * (ATTN_SCALE * LOG2E)).T.astype(BF16)
                qt_ref[0, 2 * (2 * c + e), 0:half, :] = qt[0:half]
                qt_ref[0, 2 * (2 * c + e) + 1, half:LANES, :] = qt[half:LANES]
        else:
            q16_ref[0, :, width * c:width * (c + 1)] = (res * ATTN_SCALE).astype(BF16)

    def ext_q(c, res):
        rowt = lax.broadcasted_iota(jnp.int32, (half, tm), 0)
        for e in range(2):
            et = res[:, LANES * e:LANES * (e + 1)].T[0:half]
            qt_ref[0, 2 * (2 * c + e), half:LANES, :] = jnp.where(rowt < N_EXT, et, 0.0).astype(BF16)
            qt_ref[0, 2 * (2 * c + e) + 1, 0:half, :] = (
                jnp.where((rowt >= N_EXT) & (rowt < 2 * N_EXT), et, 0.0).astype(BF16))

    def post_ext(c, res):
        n_q = D_ATTN // width
        if c < n_q:
            ext_q(c, res)
        else:
            ext_k(c - n_q, res)

    def main_job(base, c, post):
        return (lambda: jnp.dot(xb, w_ref[:, base + width * c:base + width * (c + 1)],
                                preferred_element_type=F32), functools.partial(post, c))

    heavy = [main_job(C_XBC, c, post_xbc) for c in range(D_XBC // width)]
    heavy += [main_job(C_ZS, c, functools.partial(post_gate, gs_ref)) for c in range(D_SSM // width)]
    heavy += [main_job(C_ZA, c, functools.partial(post_gate, ga_ref)) for c in range(D_ATTN // width)]
    light = [main_job(C_K, c, post_k) for c in range(D_ATTN // width)]
    light += [main_job(C_V, c, post_v) for c in range(D_ATTN // width)]
    light += [main_job(C_Q, c, post_q) for c in range(D_ATTN // width)]
    if aug:
        light += [(lambda c=c: jnp.dot(state["fa"], pext_ref[:, width * c:width * (c + 1)],
                                       preferred_element_type=F32), functools.partial(post_ext, c))
                  for c in range(2 * D_ATTN // width)]
    jobs = [(lambda: jnp.dot(xb, w_ref[:, C_SM:N_PROJ], preferred_element_type=F32), post_small)]
    while heavy or light:
        jobs += light[:1] + heavy[:1]
        light, heavy = light[1:], heavy[1:]
    ahead = 2
    pending = [jobs[n][0]() for n in range(ahead)]
    for n, (_, post) in enumerate(jobs):
        res = pending.pop(0)
        if n + ahead < len(jobs):
            pending.append(jobs[n + ahead][0]())
        post(res)

    @pl.when(t == nt - 1)
    def _():
        cout_ref[0] = xp_sc[tm + 5:tm + 8, :]


def _const_spec(shape):
    return pl.BlockSpec(shape, lambda b, t: (0,) * len(shape), pipeline_mode=pl.Buffered(1))


def _proj_call(x, wp, cst, cw, cb, sbias, alog, pext, *, tm, chunk, aug):
    bsz, tlen, _ = x.shape
    assert tlen % tm == 0 and tm % chunk == 0 and chunk & (chunk - 1) == 0
    grid = (bsz, tlen // tm)

    def tok(width):
        return pl.BlockSpec((1, tm, width), lambda b, t: (b, t, 0))

    tr = pl.BlockSpec((1, LANES, tm), lambda b, t: (b, 0, t))
    per_b = pl.BlockSpec((1, CONV_W - 1, D_XBC), lambda b, t: (b, 0, 0))
    in_specs = [tok(D_MODEL), _const_spec((D_MODEL, N_PROJ)), per_b, _const_spec((CONV_W, D_XBC)),
                _const_spec((1, D_XBC)), _const_spec((1, LANES)), _const_spec((1, LANES))]
    args = [x, wp, cst, cw, cb, sbias, alog]

    def sds(width, dtype):
        return jax.ShapeDtypeStruct((bsz, tlen, width), dtype)

    out_shape = [sds(D_ATTN, F32), sds(D_ATTN, F32)]
    out_specs = [tok(D_ATTN), tok(D_ATTN)]
    if aug:
        in_specs.append(_const_spec((LANES, 2 * D_ATTN)))
        args.append(pext)
        assert tm % FOX_TK == 0
        out_shape += [jax.ShapeDtypeStruct((bsz, N_HEADS, LANES, tlen), BF16), sds(2 * D_ATTN, BF16),
                      jax.ShapeDtypeStruct((bsz, tlen // FOX_TK, D_ATTN, FOX_TK), BF16)]
        out_specs += [pl.BlockSpec((1, N_HEADS, LANES, tm), lambda b, t: (b, 0, 0, t)), tok(2 * D_ATTN),
                      pl.BlockSpec((1, tm // FOX_TK, D_ATTN, FOX_TK), lambda b, t: (b, t, 0, 0))]
    else:
        out_shape += [sds(D_ATTN, BF16)]
        out_specs += [tok(D_ATTN)]
    out_shape += [sds(D_ATTN, BF16), sds(D_SSM, BF16), sds(D_XBC, BF16), sds(LANES, F32), sds(LANES, BF16),
                  jax.ShapeDtypeStruct((bsz, LANES, tlen), F32), jax.ShapeDtypeStruct((bsz, LANES, tlen), F32),
                  jax.ShapeDtypeStruct((bsz, CONV_W - 1, D_XBC), F32)]
    out_specs += [tok(D_ATTN), tok(D_SSM), tok(D_XBC), tok(LANES), tok(LANES), tr, tr, per_b]
    return pl.pallas_call(
        functools.partial(_proj_kernel, tm=tm, chunk=chunk, aug=aug),
        grid=grid, in_specs=in_specs, out_specs=out_specs, out_shape=out_shape,
        scratch_shapes=[pltpu.VMEM((tm + 8, D_XBC), F32), pltpu.VMEM((1, LANES), F32)],
        compiler_params=pltpu.CompilerParams(dimension_semantics=("arbitrary", "arbitrary"),
                                             vmem_limit_bytes=VMEM_LIMIT),
        name="proj_aug" if aug else "proj_dec",
    )(*args)


def _ssd_kernel(xbc_ref, gate_ref, csa_ref, sct_ref, smt_ref, h0_ref, prep_ref, pcd_ref, dskip_ref, nw_ref,
                y_ref, hout_ref, h_sc, y_sc, *, nsub):
    L = SSD_CHUNK
    t = pl.program_id(1)
    nt = pl.num_programs(1)

    @pl.when(t == 0)
    def _():
        h_sc[...] = h0_ref[0]

    lane = lax.broadcasted_iota(jnp.int32, (L, LANES), 1)
    low = lane < HEAD_DIM
    tri = lax.broadcasted_iota(jnp.int32, (L, L), 0) >= lax.broadcasted_iota(jnp.int32, (L, L), 1)
    zero16 = jnp.zeros((L, LANES), BF16)

    for j in range(nsub):
        rows = slice(j * L, (j + 1) * L)
        csa = csa_ref[0, rows, :]
        csrep = jnp.dot(csa, prep_ref[...], preferred_element_type=F32)
        cst = sct_ref[0, 0:N_HEADS, rows] * LOG2E
        dtt = smt_ref[0, 0:N_HEADS, rows]
        w2t = dtt * jnp.exp2(cst[:, L - 1:L] - cst)
        cst_dt = cst - jnp.log2(dtt)
        cd = jnp.exp2(jnp.dot(csa[L - 8:L, :], pcd_ref[...], preferred_element_type=F32)[7:8, :])
        for g in range(N_GROUPS):
            bg = xbc_ref[0, rows, D_SSM + D_STATE * g:D_SSM + D_STATE * (g + 1)]
            cg = xbc_ref[0, rows, D_SSM + D_BC + D_STATE * g:D_SSM + D_BC + D_STATE * (g + 1)]
            cb = lax.dot_general(cg, bg, (((1,), (1,)), ((), ())), preferred_element_type=F32)
            bgt = bg.astype(F32).T
            cgf = cg.astype(F32)

            def head_parts(h):
                rep = csrep[:, LANES * h:LANES * (h + 1)]
                dec = jnp.exp2(jnp.where(tri, rep - cst_dt[h:h + 1, :], -jnp.inf))
                m = (cb * dec).astype(BF16)
                ce = (cgf * jnp.exp2(rep)).astype(BF16)
                btw = (bgt * w2t[h:h + 1, :]).astype(BF16)
                return m, ce, btw

            for cc in range(N_PAIRS // N_GROUPS):
                c = (N_PAIRS // N_GROUPS) * g + cc
                cols = slice(LANES * c, LANES * (c + 1))
                xs = xbc_ref[0, rows, cols]
                xa = jnp.where(low, xs, zero16)
                xb = jnp.where(low, zero16, xs)
                hp = h_sc[:, cols]
                hp16 = hp.astype(BF16)
                ha = jnp.where(low, hp16, zero16)
                hb = jnp.where(low, zero16, hp16)
                ma, cea, btwa = head_parts(2 * c)
                mb, ceb, btwb = head_parts(2 * c + 1)
                y = jnp.dot(jnp.concatenate([ma, cea], axis=1), jnp.concatenate([xa, ha], axis=0),
                            preferred_element_type=F32)
                y = y + jnp.dot(jnp.concatenate([mb, ceb], axis=1), jnp.concatenate([xb, hb], axis=0),
                                preferred_element_type=F32)
                st = jnp.dot(jnp.concatenate([btwa, btwb], axis=1), jnp.concatenate([xa, xb], axis=0),
                             preferred_element_type=F32)
                h_sc[:, cols] = hp * cd[:, cols] + st
                y_sc[rows, cols] = y + dskip_ref[:, cols] * xs.astype(F32)

    half = D_SSM // N_GROUPS
    for g in range(N_GROUPS):
        cols = slice(half * g, half * (g + 1))
        yg = y_sc[:, cols] * gate_ref[0, :, cols].astype(F32)
        ms = jnp.mean(yg * yg, axis=-1, keepdims=True)
        y_ref[0, :, cols] = (yg * lax.rsqrt(ms + RMS_EPS) * nw_ref[:, cols]).astype(BF16)

    @pl.when(t == nt - 1)
    def _():
        hout_ref[0] = h_sc[...]


def _ssd_call(xbc, gate, csa, sct, smt, h0t, prep, pcd, dskip, nw, *, nsub):
    bsz, tlen, _ = xbc.shape
    tmc = SSD_CHUNK * nsub
    assert tlen % tmc == 0
    grid = (bsz, tlen // tmc)

    def tok(width):
        return pl.BlockSpec((1, tmc, width), lambda b, t: (b, t, 0))

    tr = pl.BlockSpec((1, LANES, tmc), lambda b, t: (b, 0, t))
    st = pl.BlockSpec((1, D_STATE, D_SSM), lambda b, t: (b, 0, 0))
    return pl.pallas_call(
        functools.partial(_ssd_kernel, nsub=nsub),
        grid=grid,
        in_specs=[tok(D_XBC), tok(D_SSM), tok(LANES), tr, tr, st,
                  _const_spec((LANES, N_HEADS * LANES)), _const_spec((LANES, D_SSM)),
                  _const_spec((1, D_SSM)), _const_spec((1, D_SSM))],
        out_specs=[tok(D_SSM), st],
        out_shape=[jax.ShapeDtypeStruct((bsz, tlen, D_SSM), BF16),
                   jax.ShapeDtypeStruct((bsz, D_STATE, D_SSM), F32)],
        scratch_shapes=[pltpu.VMEM((D_STATE, D_SSM), F32), pltpu.VMEM((tmc, D_SSM), F32)],
        compiler_params=pltpu.CompilerParams(dimension_semantics=("arbitrary", "arbitrary"),
                                             vmem_limit_bytes=VMEM_LIMIT),
        name="ssd",
    )(xbc, gate, csa, sct, smt, h0t, prep, pcd, dskip, nw)


def _fox_kernel(qt_ref, ka_ref, vt_ref, g_ref, o_ref, m_sc, l_sc, acc_sc, st_sc, *, tq, kb, ahead):
    tk = FOX_TK
    i = pl.program_id(2)
    nslab = tq // tk
    assert nslab % kb == 0
    m_sc[...] = jnp.full_like(m_sc, MASK_VALUE)
    l_sc[...] = jnp.zeros_like(l_sc)
    acc_sc[...] = jnp.zeros_like(acc_sc)
    visible = lax.broadcasted_iota(jnp.int32, (tk, tk), 0) <= lax.broadcasted_iota(jnp.int32, (tk, tk), 1)
    ones_rows = jnp.ones((16, tk), BF16)

    def scores(unit):
        j, x, slab, _ = unit
        kblk = ka_ref[0, pl.ds(pl.multiple_of(j * tk, tk), tk), LANES * x:LANES * (x + 1)]
        return jnp.dot(kblk, qt_ref[0, x, :, tk * slab:tk * (slab + 1)], preferred_element_type=F32)

    def finish(unit, st):
        j, x, slab, masked = unit
        cols = slice(tk * slab, tk * (slab + 1))
        if masked:
            st = jnp.where(visible, st, MASK_VALUE)
        m_old = m_sc[x, :, cols]
        m_new = jnp.maximum(m_old, jnp.max(st, axis=0, keepdims=True))
        p = jnp.exp2(st - m_new).astype(BF16)
        alpha = jnp.exp2(m_old - m_new)
        vt1 = jnp.concatenate([vt_ref[0, j, HEAD_DIM * x:HEAD_DIM * (x + 1), :], ones_rows], axis=0)
        pv = jnp.dot(vt1, p, preferred_element_type=F32)
        l_sc[x, :, cols] = alpha * l_sc[x, :, cols] + pv[HEAD_DIM:HEAD_DIM + 1, :]
        acc_sc[x, :, cols] = alpha * acc_sc[x, :, cols] + pv[0:HEAD_DIM, :]
        m_sc[x, :, cols] = m_new

    def run(units, next_units):
        pending = []
        for n, unit in enumerate(units):
            st = st_sc[n] if n < ahead else pending.pop(0)
            if n + ahead < len(units):
                pending.append(scores(units[n + ahead]))
            elif next_units is not None:
                st_sc[n + ahead - len(units)] = scores(next_units[n + ahead - len(units)])
            finish(unit, st)

    def full_units(jo):
        return [(kb * jo + jb, x, slab, False) for jb in range(kb) for x in range(2) for slab in range(nslab)]

    tail_units = [(nslab * i + jj, x, slab, slab == jj)
                  for jj in range(nslab) for x in range(2) for slab in range(jj, nslab)]
    assert all(f[1:3] == t[1:3] for f, t in zip(full_units(0)[:ahead], tail_units[:ahead]))
    assert ahead <= 2 * nslab
    for a, unit in enumerate(full_units(0)[:ahead]):
        st_sc[a] = scores(unit)

    def body(jo, carry):
        run(full_units(jo), full_units(jo + 1))
        return carry

    lax.fori_loop(0, (nslab * i) // kb, body, 0)
    run(tail_units, None)
    ot = jnp.concatenate([acc_sc[0] / l_sc[0], acc_sc[1] / l_sc[1]], axis=0)
    o_ref[0] = (ot.T * g_ref[0].astype(F32)).astype(BF16)


def _fox_call(qt, ka, vt, ga, *, tq, kb, ahead):
    bsz, nkv, _, tk = vt.shape
    tlen = nkv * tk
    assert tlen % tq == 0 and tq % tk == 0 and tk == FOX_TK
    grid = (bsz, N_PAIRS, tlen // tq)
    return pl.pallas_call(
        functools.partial(_fox_kernel, tq=tq, kb=kb, ahead=ahead),
        grid=grid,
        in_specs=[pl.BlockSpec((1, 2, LANES, tq), lambda b, c, i: (b, c, 0, i)),
                  pl.BlockSpec((1, tlen, 2 * LANES), lambda b, c, i: (b, 0, c)),
                  pl.BlockSpec((1, nkv, LANES, tk), lambda b, c, i: (b, 0, c, 0)),
                  pl.BlockSpec((1, tq, LANES), lambda b, c, i: (b, i, c))],
        out_specs=pl.BlockSpec((1, tq, LANES), lambda b, c, i: (b, i, c)),
        out_shape=jax.ShapeDtypeStruct((bsz, tlen, D_ATTN), BF16),
        scratch_shapes=[pltpu.VMEM((2, 1, tq), F32), pltpu.VMEM((2, 1, tq), F32),
                        pltpu.VMEM((2, HEAD_DIM, tq), F32), pltpu.VMEM((ahead, tk, tk), F32)],
        compiler_params=pltpu.CompilerParams(dimension_semantics=("arbitrary", "arbitrary", "arbitrary"),
                                             vmem_limit_bytes=VMEM_LIMIT),
        name="fox_prompt",
    )(qt, ka, vt, ga)


def _fox_dec_kernel(q_ref, kc_ref, vc_ref, kn_ref, vn_ref, lft_ref, lfc_ref, g_ref, o_ref, *, past, tn, npad):
    width = past + npad
    fkt = lft_ref[0]
    lane_w = lax.broadcasted_iota(jnp.int32, fkt.shape, 1)
    sh = 1
    while sh < width:
        fkt = fkt + jnp.where(lane_w >= sh, pltpu.roll(fkt, sh, 1), 0.0)
        sh *= 2
    fcol = lfc_ref[0]
    row_w = lax.broadcasted_iota(jnp.int32, fcol.shape, 0)
    sh = 1
    while sh < width:
        fcol = fcol + jnp.where(row_w >= sh, pltpu.roll(fcol, sh, 0), 0.0)
        sh *= 2
    fq = fcol[past:past + tn, :]
    lane = lax.broadcasted_iota(jnp.int32, (tn, LANES), 1)
    low = lane < HEAD_DIM
    causal = lax.broadcasted_iota(jnp.int32, (tn, npad), 1) <= lax.broadcasted_iota(jnp.int32, (tn, npad), 0)
    nt_dims = (((1,), (1,)), ((), ()))
    for c in range(N_PAIRS):
        cols = slice(LANES * c, LANES * (c + 1))
        q2 = q_ref[0, :, cols]
        zq = jnp.zeros_like(q2)
        kc2 = kc_ref[0, :, cols].astype(BF16)
        vc2 = vc_ref[0, :, cols].astype(BF16)
        kn2 = kn_ref[0, :, cols].astype(BF16)
        vn2 = vn_ref[0, :, cols].astype(BF16)
        outs = []
        for x in range(2):
            h = 2 * c + x
            qx = jnp.where(low, q2, zq) if x == 0 else jnp.where(low, zq, q2)
            fqh = fq[:, h:h + 1]
            s_c = lax.dot_general(qx, kc2, nt_dims, preferred_element_type=F32) + fqh - fkt[h:h + 1, 0:past]
            s_n = lax.dot_general(qx, kn2, nt_dims, preferred_element_type=F32) + fqh - fkt[h:h + 1, past:width]
            s_n = jnp.where(causal, s_n, MASK_VALUE)
            m = jnp.maximum(jnp.max(s_c, axis=-1, keepdims=True), jnp.max(s_n, axis=-1, keepdims=True))
            p_c = jnp.exp(s_c - m)
            p_n = jnp.exp(s_n - m)
            den = jnp.sum(p_c, axis=-1, keepdims=True) + jnp.sum(p_n, axis=-1, keepdims=True)
            o = jnp.dot(p_c.astype(BF16), vc2, preferred_element_type=F32)
            o = o + jnp.dot(p_n.astype(BF16), vn2, preferred_element_type=F32)
            outs.append(o / den)
        o2 = jnp.where(low, outs[0], outs[1])
        o_ref[0, :, cols] = (o2 * g_ref[0, :, cols].astype(F32)).astype(BF16)


def _fox_dec_call(q16, kc, vc, kn, vn, lft, lfc, ga, *, tn):
    bsz, past, _ = kc.shape
    npad = kn.shape[1]
    width = past + npad

    def per_b(shape):
        return pl.BlockSpec((1,) + shape, lambda b: (b, 0, 0))

    return pl.pallas_call(
        functools.partial(_fox_dec_kernel, past=past, tn=tn, npad=npad),
        grid=(bsz,),
        in_specs=[per_b((tn, D_ATTN)), per_b((past, D_ATTN)), per_b((past, D_ATTN)),
                  per_b((npad, D_ATTN)), per_b((npad, D_ATTN)),
                  per_b((N_HEADS, width)), per_b((width, LANES)), per_b((tn, D_ATTN))],
        out_specs=per_b((tn, D_ATTN)),
        out_shape=jax.ShapeDtypeStruct((bsz, tn, D_ATTN), BF16),
        compiler_params=pltpu.CompilerParams(dimension_semantics=("arbitrary",),
                                             vmem_limit_bytes=VMEM_LIMIT),
        name="fox_dec",
    )(q16, kc, vc, kn, vn, lft, lfc, ga)


def _out_kernel(ys_ref, ya_ref, x_ref, w_ref, g_ref, b_ref, o_ref, *, alpha):
    mixed = jnp.dot(ys_ref[...], w_ref[0:D_SSM, :], preferred_element_type=F32)
    mixed = mixed + jnp.dot(ya_ref[...], w_ref[D_SSM:D_SSM + D_ATTN, :], preferred_element_type=F32)
    h = alpha * x_ref[...] + mixed
    mu = jnp.mean(h, axis=-1, keepdims=True)
    hc = h - mu
    var = jnp.mean(hc * hc, axis=-1, keepdims=True)
    o_ref[...] = hc * lax.rsqrt(var + LN_EPS) * g_ref[...] + b_ref[...]


def _out_call(ys, ya, x, w16, g, b, *, alpha, tm):
    m = x.shape[0]
    assert m % tm == 0
    row = pl.BlockSpec((tm, D_MODEL), lambda i: (i, 0))

    def const(shape):
        return pl.BlockSpec(shape, lambda i: (0, 0), pipeline_mode=pl.Buffered(1))

    return pl.pallas_call(
        functools.partial(_out_kernel, alpha=alpha),
        grid=(m // tm,),
        in_specs=[row, row, row, const((D_SSM + D_ATTN, D_MODEL)), const((1, D_MODEL)), const((1, D_MODEL))],
        out_specs=row,
        out_shape=jax.ShapeDtypeStruct((m, D_MODEL), F32),
        compiler_params=pltpu.CompilerParams(dimension_semantics=("arbitrary",),
                                             vmem_limit_bytes=VMEM_LIMIT),
        name="out_proj",
    )(ys, ya, x, w16, g, b)


def _placement_constants():
    prep = np.zeros((LANES, N_HEADS * LANES), np.float32)
    pcd = np.zeros((LANES, D_SSM), np.float32)
    pext = np.zeros((LANES, 2 * D_ATTN), np.float32)
    for h in range(N_HEADS):
        for part in range(3):
            prep[16 * part + h, LANES * h:LANES * (h + 1)] = 1.0
            pcd[16 * part + h, HEAD_DIM * h:HEAD_DIM * (h + 1)] = 1.0
        base = LANES * (h // 2) + N_EXT * (h % 2)
        for part in range(3):
            pext[16 * part + h, base + part] = 1.0
            pext[SPLIT_ONE_LANE, base + 3 + part] = 1.0
            pext[SPLIT_ONE_LANE, D_ATTN + base + part] = 1.0
            pext[16 * part + h, D_ATTN + base + 3 + part] = -1.0
    return jnp.asarray(prep, BF16), jnp.asarray(pcd, BF16), jnp.asarray(pext, BF16)


def _permute_w_in(w):
    z_ssm, xbc = w[:, 0:1024], w[:, 1024:2560]
    dt, q, k, v = w[:, 2560:2576], w[:, 2576:3600], w[:, 3600:4624], w[:, 4624:5648]
    z_attn, f = w[:, 5648:6672], w[:, 6672:6688]
    pad = jnp.zeros((w.shape[0], LANES - 2 * N_HEADS), w.dtype)
    return jnp.concatenate([z_ssm, xbc, q, k, v, z_attn, dt, f, pad], axis=1).astype(BF16)


def _pad_rows(a, n, mode):
    return jnp.pad(a, ((0, 0), (0, n - a.shape[1]), (0, 0)), mode=mode)


def _state_to_kernel(h):
    return jnp.transpose(h, (0, 3, 1, 2)).reshape(h.shape[0], D_STATE, D_SSM)


def _state_from_kernel(ht):
    return jnp.transpose(ht.reshape(ht.shape[0], D_STATE, N_HEADS, HEAD_DIM), (0, 2, 3, 1))


def _layer(x, conv_state, ssm_state, past_k, past_v, past_logf, p, *, depth, prompt):
    bsz, tlen, _ = x.shape
    prep, pcd, pext = _placement_constants()
    alpha = (2 * depth) ** 0.25
    if prompt:
        tm = min(PROJ_TM, tlen)
        (k, v, qt, ka, vt, ga, gs, xbc, sm, csa, sct, smt, cout) = _proj_call(
            x, p["wp"], conv_state, p["cw"], p["cb"], p["sbias"], p["alog"], pext,
            tm=tm, chunk=SSD_CHUNK, aug=True)
        nsub = SSD_NSUB if tlen % (SSD_NSUB * SSD_CHUNK) == 0 else 1
        y_ssm, ht = _ssd_call(xbc, gs, csa, sct, smt, _state_to_kernel(ssm_state), prep, pcd,
                              p["dskip"], p["nw"], nsub=nsub)
        tq = min(FOX_TQ, tlen)
        y_attn = _fox_call(qt, ka, vt, ga, tq=tq, kb=min(FOX_KB, tq // FOX_TK), ahead=FOX_AHEAD)
    else:
        assert tlen <= SSD_CHUNK
        (k, v, q16, ga, gs, xbc, sm, csa, sct, smt, cout) = _proj_call(
            x, p["wp"], conv_state, p["cw"], p["cb"], p["sbias"], p["alog"], None,
            tm=tlen, chunk=tlen, aug=False)
        y_ssm, ht = _ssd_call(
            _pad_rows(xbc, SSD_CHUNK, "constant"), _pad_rows(gs, SSD_CHUNK, "constant"),
            _pad_rows(csa, SSD_CHUNK, "edge"),
            jnp.pad(sct, ((0, 0), (0, 0), (0, SSD_CHUNK - tlen)), mode="edge"),
            jnp.pad(smt, ((0, 0), (0, 0), (0, SSD_CHUNK - tlen))),
            _state_to_kernel(ssm_state), prep, pcd, p["dskip"], p["nw"], nsub=1)
        y_ssm = y_ssm[:, :tlen]
        past = past_k.shape[1]
        lf_all = jnp.concatenate([past_logf, sm[:, :, 16:32],
                                  jnp.zeros((bsz, LANES - tlen, N_HEADS), F32)], axis=1)
        lft = jnp.transpose(lf_all, (0, 2, 1))
        lfc = jnp.pad(lf_all, ((0, 0), (0, 0), (0, LANES - N_HEADS)))
        y_attn = _fox_dec_call(q16, past_k.reshape(bsz, past, D_ATTN).astype(BF16),
                               past_v.reshape(bsz, past, D_ATTN).astype(BF16),
                               _pad_rows(k, LANES, "constant"), _pad_rows(v, LANES, "constant"),
                               lft, lfc, ga, tn=tlen)
    m = bsz * tlen
    y = _out_call(y_ssm.reshape(m, D_SSM), y_attn.reshape(m, D_ATTN), x.reshape(m, D_MODEL),
                  p["wo"], p["ln_g"], p["ln_b"], alpha=alpha, tm=min(OUT_TM, m))
    return (y.reshape(bsz, tlen, D_MODEL), cout, _state_from_kernel(ht),
            k.reshape(bsz, tlen, N_HEADS, HEAD_DIM), v.reshape(bsz, tlen, N_HEADS, HEAD_DIM),
            jnp.swapaxes(smt[:, 16:32, :], 1, 2))


def kernel(x_prompt, x_sample, cache_k, cache_v, cache_logf, state_conv, state_ssm, w_in, conv_w, conv_b,
           dt_bias, a_log, d_skip, ssm_norm_w, f_bias, w_out, ln_g, ln_b):
    depth = w_in.shape[0]
    yp, ys = x_prompt, x_sample
    bp = x_prompt.shape[0]
    outs_p, outs_s = [], []
    for layer in range(depth):
        zpad = jnp.zeros((1, LANES - 2 * N_HEADS), F32)
        p = dict(
            wp=_permute_w_in(w_in[layer]),
            cw=conv_w[layer], cb=conv_b[layer][None, :],
            sbias=jnp.concatenate([dt_bias[layer][None, :], f_bias[layer][None, :], zpad], axis=1),
            alog=jnp.concatenate([a_log[layer][None, :], jnp.zeros((1, LANES - N_HEADS), F32)], axis=1),
            dskip=jnp.repeat(d_skip[layer], HEAD_DIM)[None, :], nw=ssm_norm_w[layer][None, :],
            wo=w_out[layer].astype(BF16), ln_g=ln_g[layer][None, :], ln_b=ln_b[layer][None, :])
        conv0 = jnp.zeros((bp, CONV_W - 1, D_XBC), F32)
        ssm0 = jnp.zeros((bp, N_HEADS, HEAD_DIM, D_STATE), state_ssm.dtype)
        yp, cp, sp, kp, vp, fp = _layer(yp, conv0, ssm0, None, None, None, p, depth=depth, prompt=True)
        ys, cs, ss, kn, vn, fn = _layer(ys, state_conv[layer], state_ssm[layer], cache_k[layer], cache_v[layer],
                                        cache_logf[layer], p, depth=depth, prompt=False)
        outs_p.append((kp, vp, fp, cp, sp))
        outs_s.append((kn, vn, fn, cs, ss))

    def stack(items, idx):
        return jnp.stack([it[idx] for it in items])

    return (yp, ys, stack(outs_p, 0), stack(outs_p, 1), stack(outs_p, 2), stack(outs_p, 3), stack(outs_p, 4),
            stack(outs_s, 0), stack(outs_s, 1), stack(outs_s, 2), stack(outs_s, 3), stack(outs_s, 4))
```

```python
import functools

import numpy as np
import jax
import jax.numpy as jnp
from jax import lax
from jax.experimental import pallas as pl
from jax.experimental.pallas import tpu as pltpu

F32 = jnp.float32
BF16 = jnp.bfloat16

D_MODEL = 1024
D_SSM = 1024
D_ATTN = 1024
HEAD_DIM = 64
N_HEADS = 16
N_PAIRS = N_HEADS // 2
N_GROUPS = 2
D_STATE = 128
D_BC = N_GROUPS * D_STATE
CONV_W = 4
D_XBC = D_SSM + 2 * D_BC
ATTN_SCALE = HEAD_DIM ** -0.5
LN_EPS = 1e-5
RMS_EPS = 1e-5
LANES = 128
SSD_CHUNK = 128
MASK_VALUE = -1e30
LOG2E = 1.4426950408889634
FOX_TK = 256
PROJ_TM = 512
SSD_NSUB = 8
FOX_TQ = 2048
FOX_KB = 8
FOX_AHEAD = 5
OUT_TM = 1024
VMEM_LIMIT = 56 * 1024 * 1024

C_ZS, C_XBC, C_Q, C_K, C_V, C_ZA, C_SM = 0, 1024, 2560, 3584, 4608, 5632, 6656
N_PROJ = C_SM + LANES
SPLIT_ONE_LANE = 3 * N_HEADS
N_EXT = 6


def _silu(z):
    return z * jax.nn.sigmoid(z)


def _split3_packed(val, lane, base):
    hi = val.astype(BF16).astype(F32)
    r1 = val - hi
    mid = r1.astype(BF16).astype(F32)
    lo = r1 - mid

    def to(v, dst):
        shift = (dst - base) % LANES
        return v if shift == 0 else pltpu.roll(v, shift, 1)

    return jnp.where(lane < 16, to(hi, 0),
                     jnp.where(lane < 32, to(mid, 16),
                               jnp.where(lane < 48, to(lo, 32), 0.0)))


def _proj_kernel(*refs, tm, chunk, aug):
    if aug:
        (x_ref, w_ref, cst_ref, cw_ref, cb_ref, sbias_ref, alog_ref, pext_ref,
         k_ref, v_ref, qt_ref, ka_ref, vt_ref, ga_ref, gs_ref, xbc_ref, sm_ref, csa_ref,
         sct_ref, smt_ref, cout_ref, xp_sc, fc_sc) = refs
    else:
        (x_ref, w_ref, cst_ref, cw_ref, cb_ref, sbias_ref, alog_ref,
         k_ref, v_ref, q16_ref, ga_ref, gs_ref, xbc_ref, sm_ref, csa_ref,
         sct_ref, smt_ref, cout_ref, xp_sc, fc_sc) = refs
    t = pl.program_id(1)
    nt = pl.num_programs(1)
    width = 2 * LANES

    @pl.when(t == 0)
    def _():
        fc_sc[...] = jnp.zeros_like(fc_sc)
        xp_sc[5:8, :] = cst_ref[0]

    @pl.when(t > 0)
    def _():
        xp_sc[5:8, :] = xp_sc[tm + 5:tm + 8, :]

    xb = x_ref[0].astype(BF16)
    lane = lax.broadcasted_iota(jnp.int32, (tm, LANES), 1)
    state = {}

    def post_small(res):
        row = lax.broadcasted_iota(jnp.int32, (tm, LANES), 0)
        u = res + sbias_ref[...]
        dt = jax.nn.softplus(u)
        logf = jax.nn.log_sigmoid(u)
        sm = jnp.where(lane < 16, dt, jnp.where(lane < 32, logf, 0.0))
        sm_ref[0] = sm
        aneg = -jnp.exp(alog_ref[...])
        s = jnp.where(lane < 16, dt * aneg, jnp.where(lane < 32, logf, 0.0))
        rowc = row & (chunk - 1)
        sh = 1
        while sh < tm:
            ok = (row >= sh) & ((lane >= 16) | (rowc >= sh))
            s = s + jnp.where(ok, pltpu.roll(s, sh, 0), 0.0)
            sh *= 2
        sc = s + fc_sc[...]
        lane1 = lax.broadcasted_iota(jnp.int32, (1, LANES), 1)
        fc_sc[...] = jnp.where((lane1 >= 16) & (lane1 < 32), sc[tm - 1:tm, :], 0.0)
        sct_ref[0] = sc.T
        smt_ref[0] = sm.T
        csa_ref[0] = _split3_packed(sc * LOG2E, lane, 0).astype(BF16)
        if aug:
            fa = _split3_packed(sc * LOG2E, lane, 16)
            state["fa"] = jnp.where(lane == SPLIT_ONE_LANE, 1.0, fa).astype(BF16)

    def post_xbc(c, res):
        xp_sc[8:8 + tm, width * c:width * (c + 1)] = res
        for j in range(2 * c, 2 * c + 2):
            sl = slice(LANES * j, LANES * (j + 1))
            acc = cb_ref[:, sl] + cw_ref[3:4, sl] * xp_sc[8:8 + tm, sl]
            acc = acc + cw_ref[2:3, sl] * xp_sc[7:7 + tm, sl]
            acc = acc + cw_ref[1:2, sl] * xp_sc[6:6 + tm, sl]
            acc = acc + cw_ref[0:1, sl] * xp_sc[5:5 + tm, sl]
            xbc_ref[0, :, sl] = _silu(acc).astype(BF16)

    def post_gate(ref, c, res):
        ref[0, :, width * c:width * (c + 1)] = _silu(res).astype(BF16)

    half = HEAD_DIM

    def post_k(c, res):
        k_ref[0, :, width * c:width * (c + 1)] = res
        if aug:
            for e in range(2):
                even = LANES * 2 * (2 * c + e)
                kcol = res[:, LANES * e:LANES * (e + 1)].astype(BF16)
                ka_ref[0, :, even:even + half] = kcol[:, 0:half]
                ka_ref[0, :, even + LANES + half:even + 2 * LANES] = kcol[:, half:LANES]

    def ext_k(c, res):
        for e in range(2):
            even = LANES * 2 * (2 * c + e)
            ext = res[:, LANES * e:LANES * (e + 1)]
            ka_ref[0, :, even + half:even + LANES] = pltpu.roll(ext, half, 1)[:, half:LANES].astype(BF16)
            ka_ref[0, :, even + LANES:even + LANES + half] = ext[:, 0:half].astype(BF16)

    def post_v(c, res):
        v_ref[0, :, width * c:width * (c + 1)] = res
        if aug:
            for r in range(tm // FOX_TK):
                vt_ref[0, r, width * c:width * (c + 1), :] = res[FOX_TK * r:FOX_TK * (r + 1), :].T.astype(BF16)

    def post_q(c, res):
        if aug:
            for e in range(2):
                qt = (res[:, LANES * e:LANES * (e + 1)] * (ATTN_SCALE * LOG2E)).T.astype(BF16)
                qt_ref[0, 2 * (2 * c + e), 0:half, :] = qt[0:half]
                qt_ref[0, 2 * (2 * c + e) + 1, half:LANES, :] = qt[half:LANES]
        else:
            q16_ref[0, :, width * c:width * (c + 1)] = (res * ATTN_SCALE).astype(BF16)

    def ext_q(c, res):
        rowt = lax.broadcasted_iota(jnp.int32, (half, tm), 0)
        for e in range(2):
            et = res[:, LANES * e:LANES * (e + 1)].T[0:half]
            qt_ref[0, 2 * (2 * c + e), half:LANES, :] = jnp.where(rowt < N_EXT, et, 0.0).astype(BF16)
            qt_ref[0, 2 * (2 * c + e) + 1, 0:half, :] = (
                jnp.where((rowt >= N_EXT) & (rowt < 2 * N_EXT), et, 0.0).astype(BF16))

    def post_ext(c, res):
        n_q = D_ATTN // width
        if c < n_q:
            ext_q(c, res)
        else:
            ext_k(c - n_q, res)

    def main_job(base, c, post):
        return (lambda: jnp.dot(xb, w_ref[:, base + width * c:base + width * (c + 1)],
                                preferred_element_type=F32), functools.partial(post, c))

    heavy = [main_job(C_XBC, c, post_xbc) for c in range(D_XBC // width)]
    heavy += [main_job(C_ZS, c, functools.partial(post_gate, gs_ref)) for c in range(D_SSM // width)]
    heavy += [main_job(C_ZA, c, functools.partial(post_gate, ga_ref)) for c in range(D_ATTN // width)]
    light = [main_job(C_K, c, post_k) for c in range(D_ATTN // width)]
    light += [main_job(C_V, c, post_v) for c in range(D_ATTN // width)]
    light += [main_job(C_Q, c, post_q) for c in range(D_ATTN // width)]
    if aug:
        light += [(lambda c=c: jnp.dot(state["fa"], pext_ref[:, width * c:width * (c + 1)],
                                       preferred_element_type=F32), functools.partial(post_ext, c))
                  for c in range(2 * D_ATTN // width)]
    jobs = [(lambda: jnp.dot(xb, w_ref[:, C_SM:N_PROJ], preferred_element_type=F32), post_small)]
    while heavy or light:
        jobs += light[:1] + heavy[:1]
        light, heavy = light[1:], heavy[1:]
    ahead = 2
    pending = [jobs[n][0]() for n in range(ahead)]
    for n, (_, post) in enumerate(jobs):
        res = pending.pop(0)
        if n + ahead < len(jobs):
            pending.append(jobs[n + ahead][0]())
        post(res)

    @pl.when(t == nt - 1)
    def _():
        cout_ref[0] = xp_sc[tm + 5:tm + 8, :]


def _const_spec(shape):
    return pl.BlockSpec(shape, lambda b, t: (0,) * len(shape), pipeline_mode=pl.Buffered(1))


def _proj_call(x, wp, cst, cw, cb, sbias, alog, pext, *, tm, chunk, aug):
    bsz, tlen, _ = x.shape
    assert tlen % tm == 0 and tm % chunk == 0 and chunk & (chunk - 1) == 0
    grid = (bsz, tlen // tm)

    def tok(width):
        return pl.BlockSpec((1, tm, width), lambda b, t: (b, t, 0))

    tr = pl.BlockSpec((1, LANES, tm), lambda b, t: (b, 0, t))
    per_b = pl.BlockSpec((1, CONV_W - 1, D_XBC), lambda b, t: (b, 0, 0))
    in_specs = [tok(D_MODEL), _const_spec((D_MODEL, N_PROJ)), per_b, _const_spec((CONV_W, D_XBC)),
                _const_spec((1, D_XBC)), _const_spec((1, LANES)), _const_spec((1, LANES))]
    args = [x, wp, cst, cw, cb, sbias, alog]

    def sds(width, dtype):
        return jax.ShapeDtypeStruct((bsz, tlen, width), dtype)

    out_shape = [sds(D_ATTN, F32), sds(D_ATTN, F32)]
    out_specs = [tok(D_ATTN), tok(D_ATTN)]
    if aug:
        in_specs.append(_const_spec((LANES, 2 * D_ATTN)))
        args.append(pext)
        assert tm % FOX_TK == 0
        out_shape += [jax.ShapeDtypeStruct((bsz, N_HEADS, LANES, tlen), BF16), sds(2 * D_ATTN, BF16),
                      jax.ShapeDtypeStruct((bsz, tlen // FOX_TK, D_ATTN, FOX_TK), BF16)]
        out_specs += [pl.BlockSpec((1, N_HEADS, LANES, tm), lambda b, t: (b, 0, 0, t)), tok(2 * D_ATTN),
                      pl.BlockSpec((1, tm // FOX_TK, D_ATTN, FOX_TK), lambda b, t: (b, t, 0, 0))]
    else:
        out_shape += [sds(D_ATTN, BF16)]
        out_specs += [tok(D_ATTN)]
    out_shape += [sds(D_ATTN, BF16), sds(D_SSM, BF16), sds(D_XBC, BF16), sds(LANES, F32), sds(LANES, BF16),
                  jax.ShapeDtypeStruct((bsz, LANES, tlen), F32), jax.ShapeDtypeStruct((bsz, LANES, tlen), F32),
                  jax.ShapeDtypeStruct((bsz, CONV_W - 1, D_XBC), F32)]
    out_specs += [tok(D_ATTN), tok(D_SSM), tok(D_XBC), tok(LANES), tok(LANES), tr, tr, per_b]
    return pl.pallas_call(
        functools.partial(_proj_kernel, tm=tm, chunk=chunk, aug=aug),
        grid=grid, in_specs=in_specs, out_specs=out_specs, out_shape=out_shape,
        scratch_shapes=[pltpu.VMEM((tm + 8, D_XBC), F32), pltpu.VMEM((1, LANES), F32)],
        compiler_params=pltpu.CompilerParams(dimension_semantics=("arbitrary", "arbitrary"),
                                             vmem_limit_bytes=VMEM_LIMIT),
        name="proj_aug" if aug else "proj_dec",
    )(*args)


def _ssd_kernel(xbc_ref, gate_ref, csa_ref, sct_ref, smt_ref, h0_ref, prep_ref, pcd_ref, dskip_ref, nw_ref,
                y_ref, hout_ref, h_sc, y_sc, *, nsub):
    L = SSD_CHUNK
    t = pl.program_id(1)
    nt = pl.num_programs(1)

    @pl.when(t == 0)
    def _():
        h_sc[...] = h0_ref[0]

    lane = lax.broadcasted_iota(jnp.int32, (L, LANES), 1)
    low = lane < HEAD_DIM
    tri = lax.broadcasted_iota(jnp.int32, (L, L), 0) >= lax.broadcasted_iota(jnp.int32, (L, L), 1)
    zero16 = jnp.zeros((L, LANES), BF16)

    for j in range(nsub):
        rows = slice(j * L, (j + 1) * L)
        csa = csa_ref[0, rows, :]
        csrep = jnp.dot(csa, prep_ref[...], preferred_element_type=F32)
        cst = sct_ref[0, 0:N_HEADS, rows] * LOG2E
        dtt = smt_ref[0, 0:N_HEADS, rows]
        w2t = dtt * jnp.exp2(cst[:, L - 1:L] - cst)
        cst_dt = cst - jnp.log2(dtt)
        cd = jnp.exp2(jnp.dot(csa[L - 8:L, :], pcd_ref[...], preferred_element_type=F32)[7:8, :])
        for g in range(N_GROUPS):
            bg = xbc_ref[0, rows, D_SSM + D_STATE * g:D_SSM + D_STATE * (g + 1)]
            cg = xbc_ref[0, rows, D_SSM + D_BC + D_STATE * g:D_SSM + D_BC + D_STATE * (g + 1)]
            cb = lax.dot_general(cg, bg, (((1,), (1,)), ((), ())), preferred_element_type=F32)
            bgt = bg.astype(F32).T
            cgf = cg.astype(F32)

            def head_parts(h):
                rep = csrep[:, LANES * h:LANES * (h + 1)]
                dec = jnp.exp2(jnp.where(tri, rep - cst_dt[h:h + 1, :], -jnp.inf))
                m = (cb * dec).astype(BF16)
                ce = (cgf * jnp.exp2(rep)).astype(BF16)
                btw = (bgt * w2t[h:h + 1, :]).astype(BF16)
                return m, ce, btw

            for cc in range(N_PAIRS // N_GROUPS):
                c = (N_PAIRS // N_GROUPS) * g + cc
                cols = slice(LANES * c, LANES * (c + 1))
                xs = xbc_ref[0, rows, cols]
                xa = jnp.where(low, xs, zero16)
                xb = jnp.where(low, zero16, xs)
                hp = h_sc[:, cols]
                hp16 = hp.astype(BF16)
                ha = jnp.where(low, hp16, zero16)
                hb = jnp.where(low, zero16, hp16)
                ma, cea, btwa = head_parts(2 * c)
                mb, ceb, btwb = head_parts(2 * c + 1)
                y = jnp.dot(jnp.concatenate([ma, cea], axis=1), jnp.concatenate([xa, ha], axis=0),
                            preferred_element_type=F32)
                y = y + jnp.dot(jnp.concatenate([mb, ceb], axis=1), jnp.concatenate([xb, hb], axis=0),
                                preferred_element_type=F32)
                st = jnp.dot(jnp.concatenate([btwa, btwb], axis=1), jnp.concatenate([xa, xb], axis=0),
                             preferred_element_type=F32)
                h_sc[:, cols] = hp * cd[:, cols] + st
                y_sc[rows, cols] = y + dskip_ref[:, cols] * xs.astype(F32)

    half = D_SSM // N_GROUPS
    for g in range(N_GROUPS):
        cols = slice(half * g, half * (g + 1))
        yg = y_sc[:, cols] * gate_ref[0, :, cols].astype(F32)
        ms = jnp.mean(yg * yg, axis=-1, keepdims=True)
        y_ref[0, :, cols] = (yg * lax.rsqrt(ms + RMS_EPS) * nw_ref[:, cols]).astype(BF16)

    @pl.when(t == nt - 1)
    def _():
        hout_ref[0] = h_sc[...]


def _ssd_call(xbc, gate, csa, sct, smt, h0t, prep, pcd, dskip, nw, *, nsub):
    bsz, tlen, _ = xbc.shape
    tmc = SSD_CHUNK * nsub
    assert tlen % tmc == 0
    grid = (bsz, tlen // tmc)

    def tok(width):
        return pl.BlockSpec((1, tmc, width), lambda b, t: (b, t, 0))

    tr = pl.BlockSpec((1, LANES, tmc), lambda b, t: (b, 0, t))
    st = pl.BlockSpec((1, D_STATE, D_SSM), lambda b, t: (b, 0, 0))
    return pl.pallas_call(
        functools.partial(_ssd_kernel, nsub=nsub),
        grid=grid,
        in_specs=[tok(D_XBC), tok(D_SSM), tok(LANES), tr, tr, st,
                  _const_spec((LANES, N_HEADS * LANES)), _const_spec((LANES, D_SSM)),
                  _const_spec((1, D_SSM)), _const_spec((1, D_SSM))],
        out_specs=[tok(D_SSM), st],
        out_shape=[jax.ShapeDtypeStruct((bsz, tlen, D_SSM), BF16),
                   jax.ShapeDtypeStruct((bsz, D_STATE, D_SSM), F32)],
        scratch_shapes=[pltpu.VMEM((D_STATE, D_SSM), F32), pltpu.VMEM((tmc, D_SSM), F32)],
        compiler_params=pltpu.CompilerParams(dimension_semantics=("arbitrary", "arbitrary"),
                                             vmem_limit_bytes=VMEM_LIMIT),
        name="ssd",
    )(xbc, gate, csa, sct, smt, h0t, prep, pcd, dskip, nw)


def _fox_kernel(qt_ref, ka_ref, vt_ref, g_ref, o_ref, m_sc, l_sc, acc_sc, st_sc, *, tq, kb, ahead):
    tk = FOX_TK
    i = pl.program_id(2)
    nslab = tq // tk
    assert nslab % kb == 0
    m_sc[...] = jnp.full_like(m_sc, MASK_VALUE)
    l_sc[...] = jnp.zeros_like(l_sc)
    acc_sc[...] = jnp.zeros_like(acc_sc)
    visible = lax.broadcasted_iota(jnp.int32, (tk, tk), 0) <= lax.broadcasted_iota(jnp.int32, (tk, tk), 1)
    ones_rows = jnp.ones((16, tk), BF16)

    def scores(unit):
        j, x, slab, _ = unit
        kblk = ka_ref[0, pl.ds(pl.multiple_of(j * tk, tk), tk), LANES * x:LANES * (x + 1)]
        return jnp.dot(kblk, qt_ref[0, x, :, tk * slab:tk * (slab + 1)], preferred_element_type=F32)

    def finish(unit, st):
        j, x, slab, masked = unit
        cols = slice(tk * slab, tk * (slab + 1))
        if masked:
            st = jnp.where(visible, st, MASK_VALUE)
        m_old = m_sc[x, :, cols]
        m_new = jnp.maximum(m_old, jnp.max(st, axis=0, keepdims=True))
        p = jnp.exp2(st - m_new).astype(BF16)
        alpha = jnp.exp2(m_old - m_new)
        vt1 = jnp.concatenate([vt_ref[0, j, HEAD_DIM * x:HEAD_DIM * (x + 1), :], ones_rows], axis=0)
        pv = jnp.dot(vt1, p, preferred_element_type=F32)
        l_sc[x, :, cols] = alpha * l_sc[x, :, cols] + pv[HEAD_DIM:HEAD_DIM + 1, :]
        acc_sc[x, :, cols] = alpha * acc_sc[x, :, cols] + pv[0:HEAD_DIM, :]
        m_sc[x, :, cols] = m_new

    def run(units, next_units):
        pending = []
        for n, unit in enumerate(units):
            st = st_sc[n] if n < ahead else pending.pop(0)
            if n + ahead < len(units):
                pending.append(scores(units[n + ahead]))
            elif next_units is not None:
                st_sc[n + ahead - len(units)] = scores(next_units[n + ahead - len(units)])
            finish(unit, st)

    def full_units(jo):
        return [(kb * jo + jb, x, slab, False) for jb in range(kb) for x in range(2) for slab in range(nslab)]

    tail_units = [(nslab * i + jj, x, slab, slab == jj)
                  for jj in range(nslab) for x in range(2) for slab in range(jj, nslab)]
    assert all(f[1:3] == t[1:3] for f, t in zip(full_units(0)[:ahead], tail_units[:ahead]))
    assert ahead <= 2 * nslab
    for a, unit in enumerate(full_units(0)[:ahead]):
        st_sc[a] = scores(unit)

    def body(jo, carry):
        run(full_units(jo), full_units(jo + 1))
        return carry

    lax.fori_loop(0, (nslab * i) // kb, body, 0)
    run(tail_units, None)
    ot = jnp.concatenate([acc_sc[0] / l_sc[0], acc_sc[1] / l_sc[1]], axis=0)
    o_ref[0] = (ot.T * g_ref[0].astype(F32)).astype(BF16)


def _fox_call(qt, ka, vt, ga, *, tq, kb, ahead):
    bsz, nkv, _, tk = vt.shape
    tlen = nkv * tk
    assert tlen % tq == 0 and tq % tk == 0 and tk == FOX_TK
    grid = (bsz, N_PAIRS, tlen // tq)
    return pl.pallas_call(
        functools.partial(_fox_kernel, tq=tq, kb=kb, ahead=ahead),
        grid=grid,
        in_specs=[pl.BlockSpec((1, 2, LANES, tq), lambda b, c, i: (b, c, 0, i)),
                  pl.BlockSpec((1, tlen, 2 * LANES), lambda b, c, i: (b, 0, c)),
                  pl.BlockSpec((1, nkv, LANES, tk), lambda b, c, i: (b, 0, c, 0)),
                  pl.BlockSpec((1, tq, LANES), lambda b, c, i: (b, i, c))],
        out_specs=pl.BlockSpec((1, tq, LANES), lambda b, c, i: (b, i, c)),
        out_shape=jax.ShapeDtypeStruct((bsz, tlen, D_ATTN), BF16),
        scratch_shapes=[pltpu.VMEM((2, 1, tq), F32), pltpu.VMEM((2, 1, tq), F32),
                        pltpu.VMEM((2, HEAD_DIM, tq), F32), pltpu.VMEM((ahead, tk, tk), F32)],
        compiler_params=pltpu.CompilerParams(dimension_semantics=("arbitrary", "arbitrary", "arbitrary"),
                                             vmem_limit_bytes=VMEM_LIMIT),
        name="fox_prompt",
    )(qt, ka, vt, ga)


def _fox_dec_kernel(q_ref, kc_ref, vc_ref, kn_ref, vn_ref, lft_ref, lfc_ref, g_ref, o_ref, *, past, tn, npad):
    width = past + npad
    fkt = lft_ref[0]
    lane_w = lax.broadcasted_iota(jnp.int32, fkt.shape, 1)
    sh = 1
    while sh < width:
        fkt = fkt + jnp.where(lane_w >= sh, pltpu.roll(fkt, sh, 1), 0.0)
        sh *= 2
    fcol = lfc_ref[0]
    row_w = lax.broadcasted_iota(jnp.int32, fcol.shape, 0)
    sh = 1
    while sh < width:
        fcol = fcol + jnp.where(row_w >= sh, pltpu.roll(fcol, sh, 0), 0.0)
        sh *= 2
    fq = fcol[past:past + tn, :]
    lane = lax.broadcasted_iota(jnp.int32, (tn, LANES), 1)
    low = lane < HEAD_DIM
    causal = lax.broadcasted_iota(jnp.int32, (tn, npad), 1) <= lax.broadcasted_iota(jnp.int32, (tn, npad), 0)
    nt_dims = (((1,), (1,)), ((), ()))
    for c in range(N_PAIRS):
        cols = slice(LANES * c, LANES * (c + 1))
        q2 = q_ref[0, :, cols]
        zq = jnp.zeros_like(q2)
        kc2 = kc_ref[0, :, cols].astype(BF16)
        vc2 = vc_ref[0, :, cols].astype(BF16)
        kn2 = kn_ref[0, :, cols].astype(BF16)
        vn2 = vn_ref[0, :, cols].astype(BF16)
        outs = []
        for x in range(2):
            h = 2 * c + x
            qx = jnp.where(low, q2, zq) if x == 0 else jnp.where(low, zq, q2)
            fqh = fq[:, h:h + 1]
            s_c = lax.dot_general(qx, kc2, nt_dims, preferred_element_type=F32) + fqh - fkt[h:h + 1, 0:past]
            s_n = lax.dot_general(qx, kn2, nt_dims, preferred_element_type=F32) + fqh - fkt[h:h + 1, past:width]
            s_n = jnp.where(causal, s_n, MASK_VALUE)
            m = jnp.maximum(jnp.max(s_c, axis=-1, keepdims=True), jnp.max(s_n, axis=-1, keepdims=True))
            p_c = jnp.exp(s_c - m)
            p_n = jnp.exp(s_n - m)
            den = jnp.sum(p_c, axis=-1, keepdims=True) + jnp.sum(p_n, axis=-1, keepdims=True)
            o = jnp.dot(p_c.astype(BF16), vc2, preferred_element_type=F32)
            o = o + jnp.dot(p_n.astype(BF16), vn2, preferred_element_type=F32)
            outs.append(o / den)
        o2 = jnp.where(low, outs[0], outs[1])
        o_ref[0, :, cols] = (o2 * g_ref[0, :, cols].astype(F32)).astype(BF16)


def _fox_dec_call(q16, kc, vc, kn, vn, lft, lfc, ga, *, tn):
    bsz, past, _ = kc.shape
    npad = kn.shape[1]
    width = past + npad

    def per_b(shape):
        return pl.BlockSpec((1,) + shape, lambda b: (b, 0, 0))

    return pl.pallas_call(
        functools.partial(_fox_dec_kernel, past=past, tn=tn, npad=npad),
        grid=(bsz,),
        in_specs=[per_b((tn, D_ATTN)), per_b((past, D_ATTN)), per_b((past, D_ATTN)),
                  per_b((npad, D_ATTN)), per_b((npad, D_ATTN)),
                  per_b((N_HEADS, width)), per_b((width, LANES)), per_b((tn, D_ATTN))],
        out_specs=per_b((tn, D_ATTN)),
        out_shape=jax.ShapeDtypeStruct((bsz, tn, D_ATTN), BF16),
        compiler_params=pltpu.CompilerParams(dimension_semantics=("arbitrary",),
                                             vmem_limit_bytes=VMEM_LIMIT),
        name="fox_dec",
    )(q16, kc, vc, kn, vn, lft, lfc, ga)


def _out_kernel(ys_ref, ya_ref, x_ref, w_ref, g_ref, b_ref, o_ref, *, alpha):
    mixed = jnp.dot(ys_ref[...], w_ref[0:D_SSM, :], preferred_element_type=F32)
    mixed = mixed + jnp.dot(ya_ref[...], w_ref[D_SSM:D_SSM + D_ATTN, :], preferred_element_type=F32)
    h = alpha * x_ref[...] + mixed
    mu = jnp.mean(h, axis=-1, keepdims=True)
    hc = h - mu
    var = jnp.mean(hc * hc, axis=-1, keepdims=True)
    o_ref[...] = hc * lax.rsqrt(var + LN_EPS) * g_ref[...] + b_ref[...]


def _out_call(ys, ya, x, w16, g, b, *, alpha, tm):
    m = x.shape[0]
    assert m % tm == 0
    row = pl.BlockSpec((tm, D_MODEL), lambda i: (i, 0))

    def const(shape):
        return pl.BlockSpec(shape, lambda i: (0, 0), pipeline_mode=pl.Buffered(1))

    return pl.pallas_call(
        functools.partial(_out_kernel, alpha=alpha),
        grid=(m // tm,),
        in_specs=[row, row, row, const((D_SSM + D_ATTN, D_MODEL)), const((1, D_MODEL)), const((1, D_MODEL))],
        out_specs=row,
        out_shape=jax.ShapeDtypeStruct((m, D_MODEL), F32),
        compiler_params=pltpu.CompilerParams(dimension_semantics=("arbitrary",),
                                             vmem_limit_bytes=VMEM_LIMIT),
        name="out_proj",
    )(ys, ya, x, w16, g, b)


def _placement_constants():
    prep = np.zeros((LANES, N_HEADS * LANES), np.float32)
    pcd = np.zeros((LANES, D_SSM), np.float32)
    pext = np.zeros((LANES, 2 * D_ATTN), np.float32)
    for h in range(N_HEADS):
        for part in range(3):
            prep[16 * part + h, LANES * h:LANES * (h + 1)] = 1.0
            pcd[16 * part + h, HEAD_DIM * h:HEAD_DIM * (h + 1)] = 1.0
        base = LANES * (h // 2) + N_EXT * (h % 2)
        for part in range(3):
            pext[16 * part + h, base + part] = 1.0
            pext[SPLIT_ONE_LANE, base + 3 + part] = 1.0
            pext[SPLIT_ONE_LANE, D_ATTN + base + part] = 1.0
            pext[16 * part + h, D_ATTN + base + 3 + part] = -1.0
    return jnp.asarray(prep, BF16), jnp.asarray(pcd, BF16), jnp.asarray(pext, BF16)


def _permute_w_in(w):
    z_ssm, xbc = w[:, 0:1024], w[:, 1024:2560]
    dt, q, k, v = w[:, 2560:2576], w[:, 2576:3600], w[:, 3600:4624], w[:, 4624:5648]
    z_attn, f = w[:, 5648:6672], w[:, 6672:6688]
    pad = jnp.zeros((w.shape[0], LANES - 2 * N_HEADS), w.dtype)
    return jnp.concatenate([z_ssm, xbc, q, k, v, z_attn, dt, f, pad], axis=1).astype(BF16)


def _pad_rows(a, n, mode):
    return jnp.pad(a, ((0, 0), (0, n - a.shape[1]), (0, 0)), mode=mode)


def _state_to_kernel(h):
    return jnp.transpose(h, (0, 3, 1, 2)).reshape(h.shape[0], D_STATE, D_SSM)


def _state_from_kernel(ht):
    return jnp.transpose(ht.reshape(ht.shape[0], D_STATE, N_HEADS, HEAD_DIM), (0, 2, 3, 1))


def _layer(x, conv_state, ssm_state, past_k, past_v, past_logf, p, *, depth, prompt):
    bsz, tlen, _ = x.shape
    prep, pcd, pext = _placement_constants()
    alpha = (2 * depth) ** 0.25
    if prompt:
        tm = min(PROJ_TM, tlen)
        (k, v, qt, ka, vt, ga, gs, xbc, sm, csa, sct, smt, cout) = _proj_call(
            x, p["wp"], conv_state, p["cw"], p["cb"], p["sbias"], p["alog"], pext,
            tm=tm, chunk=SSD_CHUNK, aug=True)
        nsub = SSD_NSUB if tlen % (SSD_NSUB * SSD_CHUNK) == 0 else 1
        y_ssm, ht = _ssd_call(xbc, gs, csa, sct, smt, _state_to_kernel(ssm_state), prep, pcd,
                              p["dskip"], p["nw"], nsub=nsub)
        tq = min(FOX_TQ, tlen)
        y_attn = _fox_call(qt, ka, vt, ga, tq=tq, kb=min(FOX_KB, tq // FOX_TK), ahead=FOX_AHEAD)
    else:
        assert tlen <= SSD_CHUNK
        (k, v, q16, ga, gs, xbc, sm, csa, sct, smt, cout) = _proj_call(
            x, p["wp"], conv_state, p["cw"], p["cb"], p["sbias"], p["alog"], None,
            tm=tlen, chunk=tlen, aug=False)
        y_ssm, ht = _ssd_call(
            _pad_rows(xbc, SSD_CHUNK, "constant"), _pad_rows(gs, SSD_CHUNK, "constant"),
            _pad_rows(csa, SSD_CHUNK, "edge"),
            jnp.pad(sct, ((0, 0), (0, 0), (0, SSD_CHUNK - tlen)), mode="edge"),
            jnp.pad(smt, ((0, 0), (0, 0), (0, SSD_CHUNK - tlen))),
            _state_to_kernel(ssm_state), prep, pcd, p["dskip"], p["nw"], nsub=1)
        y_ssm = y_ssm[:, :tlen]
        past = past_k.shape[1]
        lf_all = jnp.concatenate([past_logf, sm[:, :, 16:32],
                                  jnp.zeros((bsz, LANES - tlen, N_HEADS), F32)], axis=1)
        lft = jnp.transpose(lf_all, (0, 2, 1))
        lfc = jnp.pad(lf_all, ((0, 0), (0, 0), (0, LANES - N_HEADS)))
        y_attn = _fox_dec_call(q16, past_k.reshape(bsz, past, D_ATTN), past_v.reshape(bsz, past, D_ATTN),
                               _pad_rows(k, LANES, "constant"), _pad_rows(v, LANES, "constant"),
                               lft, lfc, ga, tn=tlen)
    m = bsz * tlen
    y = _out_call(y_ssm.reshape(m, D_SSM), y_attn.reshape(m, D_ATTN), x.reshape(m, D_MODEL),
                  p["wo"], p["ln_g"], p["ln_b"], alpha=alpha, tm=min(OUT_TM, m))
    return (y.reshape(bsz, tlen, D_MODEL), cout, _state_from_kernel(ht),
            k.reshape(bsz, tlen, N_HEADS, HEAD_DIM), v.reshape(bsz, tlen, N_HEADS, HEAD_DIM),
            jnp.swapaxes(smt[:, 16:32, :], 1, 2))


def kernel(x_prompt, x_sample, cache_k, cache_v, cache_logf, state_conv, state_ssm, w_in, conv_w, conv_b,
           dt_bias, a_log, d_skip, ssm_norm_w, f_bias, w_out, ln_g, ln_b):
    depth = w_in.shape[0]
    yp, ys = x_prompt, x_sample
    bp = x_prompt.shape[0]
    outs_p, outs_s = [], []
    for layer in range(depth):
        zpad = jnp.zeros((1, LANES - 2 * N_HEADS), F32)
        p = dict(
            wp=_permute_w_in(w_in[layer]),
            cw=conv_w[layer], cb=conv_b[layer][None, :],
            sbias=jnp.concatenate([dt_bias[layer][None, :], f_bias[layer][None, :], zpad], axis=1),
            alog=jnp.concatenate([a_log[layer][None, :], jnp.zeros((1, LANES - N_HEADS), F32)], axis=1),
            dskip=jnp.repeat(d_skip[layer], HEAD_DIM)[None, :], nw=ssm_norm_w[layer][None, :],
            wo=w_out[layer].astype(BF16), ln_g=ln_g[layer][None, :], ln_b=ln_b[layer][None, :])
        conv0 = jnp.zeros((bp, CONV_W - 1, D_XBC), F32)
        ssm0 = jnp.zeros((bp, N_HEADS, HEAD_DIM, D_STATE), state_ssm.dtype)
        yp, cp, sp, kp, vp, fp = _layer(yp, conv0, ssm0, None, None, None, p, depth=depth, prompt=True)
        ys, cs, ss, kn, vn, fn = _layer(ys, state_conv[layer], state_ssm[layer], cache_k[layer], cache_v[layer],
                                        cache_logf[layer], p, depth=depth, prompt=False)
        outs_p.append((kp, vp, fp, cp, sp))
        outs_s.append((kn, vn, fn, cs, ss))

    def stack(items, idx):
        return jnp.stack([it[idx] for it in items])

    return (yp, ys, stack(outs_p, 0), stack(outs_p, 1), stack(outs_p, 2), stack(outs_p, 3), stack(outs_p, 4),
            stack(outs_s, 0), stack(outs_s, 1), stack(outs_s, 2), stack(outs_s, 3), stack(outs_s, 4))
```
